```python
import math
import jax
import jax.numpy as jnp
from jax import lax
import numpy as np

D_MODEL = 1024
BATCH = 8
SEQ = 4096
DEPTH = 2
DEC_BATCH = 8
DEC_SEQ = 8192
PAST_LEN = 128

D_FF = 2816
DN_HEADS = 4
DN_DK = 128
DN_DV = 128
DN_CONV = 5
DN_CHUNK = 64
CV_CH = D_MODEL // 4
CV_KERNEL = 31
AT_HEADS = 4
AT_KV_HEADS = 2
AT_HEAD_DIM = 64
AT_WINDOW = 128
ALIBI_MAX = 8.0
EPS = 1e-6
MIX_OUT = DN_HEADS * DN_DV + CV_CH + AT_HEADS * AT_HEAD_DIM
IN_SIZES = (2 * DN_HEADS * DN_DK + DN_HEADS * DN_DV,
            DN_HEADS * DN_DV,
            2 * DN_HEADS,
            2 * DN_HEADS,
            2 * CV_CH,
            AT_HEADS * AT_HEAD_DIM,
            2 * AT_KV_HEADS * AT_HEAD_DIM)
IN_COLS = sum(IN_SIZES)

kernel_name = 'hybrid_bidir_encoder_deltanet_conformer_swa'


def rms_norm(x, g):
    xf = x.astype(jnp.float32)
    y = xf * lax.rsqrt(jnp.mean(xf * xf, axis=-1, keepdims=True) + EPS)
    return (y * g.astype(jnp.float32)).astype(x.dtype)


def layer_norm(x, g, b):
    xf = x.astype(jnp.float32)
    xc = xf - jnp.mean(xf, axis=-1, keepdims=True)
    var = jnp.mean(xc * xc, axis=-1, keepdims=True)
    y = xc * lax.rsqrt(var + EPS) * g.astype(jnp.float32) + b.astype(jnp.float32)
    return y.astype(x.dtype)


def l2_norm(x):
    return x * lax.rsqrt(jnp.sum(x * x, axis=-1, keepdims=True) + EPS)


def swiglu(x, w_gate, w_up, w_down):
    return (jax.nn.silu(x @ w_gate) * (x @ w_up)) @ w_down


def depthwise_conv_centred(x, w):
    k = w.shape[0]
    left = (k - 1) // 2
    return lax.conv_general_dilated(
        x, w[:, None, :].astype(x.dtype), window_strides=(1,),
        padding=[(left, k - 1 - left)], dimension_numbers=('NWC', 'WIO', 'NWC'),
        feature_group_count=x.shape[-1])


def gated_delta_rule_chunked(q, k, v, g, beta):
    B, T, H, DK = q.shape
    DV = v.shape[-1]
    C = DN_CHUNK
    N = T // C
    q = q * (DK ** -0.5)

    def to_chunks(t):
        t = t.reshape((B, N, C, H) + t.shape[3:])
        return jnp.moveaxis(t, 3, 2)

    q, k, v, g, beta = map(to_chunks, (q, k, v, g, beta))
    gc = jnp.cumsum(g, axis=-1)
    idx = jnp.arange(C)
    incl = idx[:, None] >= idx[None, :]
    strict = idx[:, None] > idx[None, :]
    decay = jnp.exp(jnp.where(incl, gc[..., :, None] - gc[..., None, :], -jnp.inf))
    kb = k * beta[..., None]
    vb = v * beta[..., None]
    a_mat = jnp.where(strict, jnp.einsum('bnhcd,bnhsd->bnhcs', kb, k) * decay, 0.0)
    eye = jnp.eye(C, dtype=q.dtype)
    t_mat = lax.linalg.triangular_solve(a_mat + eye, jnp.broadcast_to(eye, a_mat.shape),
                                        left_side=True, lower=True)
    u = jnp.einsum('bnhcs,bnhse->bnhce', t_mat, vb)
    w = jnp.einsum('bnhcs,bnhsd->bnhcd', t_mat, kb * jnp.exp(gc)[..., None])
    qk = jnp.einsum('bnhcd,bnhsd->bnhcs', q, k) * decay
    q_dec = q * jnp.exp(gc)[..., None]
    k_dec = k * jnp.exp(gc[..., -1:] - gc)[..., None]
    g_last = jnp.exp(gc[..., -1])
    xs = tuple(jnp.moveaxis(t, 1, 0) for t in (qk, u, w, q_dec, k_dec, g_last))

    def step(s, inp):
        qk_i, u_i, w_i, qd_i, kd_i, gl_i = inp
        v_new = u_i - jnp.einsum('bhcd,bhde->bhce', w_i, s)
        o = jnp.einsum('bhcd,bhde->bhce', qd_i, s) + jnp.einsum('bhcs,bhse->bhce', qk_i, v_new)
        s = s * gl_i[..., None, None] + jnp.einsum('bhcd,bhce->bhde', kd_i, v_new)
        return s, o

    s0 = jnp.zeros((B, H, DK, DV), q.dtype)
    _, o = lax.scan(step, s0, xs)
    o = jnp.moveaxis(o, 0, 1).transpose(0, 1, 3, 2, 4)
    return o.reshape(B, T, H, DV)


def deltanet_mixer(qkv, z, b_logit, a_logit, conv_w, a_log, dt_bias, out_g):
    B, T, _ = qkv.shape
    f32 = jnp.float32
    qkv = jax.nn.silu(depthwise_conv_centred(qkv, conv_w)).astype(f32)
    q, k, v = jnp.split(qkv, [DN_HEADS * DN_DK, 2 * DN_HEADS * DN_DK], axis=-1)
    q = l2_norm(q.reshape(B, T, DN_HEADS, DN_DK))
    k = l2_norm(k.reshape(B, T, DN_HEADS, DN_DK))
    v = v.reshape(B, T, DN_HEADS, DN_DV)
    beta = jax.nn.sigmoid(b_logit.astype(f32)).reshape(B, T, 2, DN_HEADS)
    g = -jnp.exp(a_log.astype(f32)) * jax.nn.softplus(
        a_logit.astype(f32).reshape(B, T, 2, DN_HEADS) + dt_bias.astype(f32))
    o_fwd = gated_delta_rule_chunked(q, k, v, g[:, :, 0], beta[:, :, 0])
    flip = lambda t: jnp.flip(t, axis=1)
    o_bwd = flip(gated_delta_rule_chunked(flip(q), flip(k), flip(v),
                                          flip(g[:, :, 1]), flip(beta[:, :, 1])))
    o = rms_norm(o_fwd + o_bwd, out_g) * jax.nn.silu(z.astype(f32).reshape(B, T, DN_HEADS, DN_DV))
    return o.reshape(B, T, DN_HEADS * DN_DV)


def conformer_conv_mixer(u, dw, dw_b, ln_g, ln_b):
    a, gate = jnp.split(u, 2, axis=-1)
    h = a * jax.nn.sigmoid(gate)
    h = depthwise_conv_centred(h, dw) + dw_b
    return jax.nn.silu(layer_norm(h, ln_g, ln_b))


def windowed_gqa_sink(q, kv, q_g, k_g, sink):
    B, T, _ = q.shape
    f32 = jnp.float32
    W, G, D = AT_WINDOW, AT_HEADS // AT_KV_HEADS, AT_HEAD_DIM
    N = T // W
    q = rms_norm(q.astype(f32).reshape(B, T, AT_HEADS, D), q_g)
    k, v = jnp.split(kv.astype(f32), 2, axis=-1)
    k = rms_norm(k.reshape(B, T, AT_KV_HEADS, D), k_g)
    v = v.reshape(B, T, AT_KV_HEADS, D)

    def bands(t):
        tp = jnp.pad(t, ((0, 0), (W, W), (0, 0), (0, 0))).reshape(B, N + 2, W, AT_KV_HEADS, D)
        return jnp.concatenate([tp[:, :N], tp[:, 1:N + 1], tp[:, 2:]], axis=2)

    kb, vb = bands(k), bands(v)
    qb = q.reshape(B, N, W, AT_KV_HEADS, G, D)
    s = jnp.einsum('bnqhgd,bnkhd->bnhgqk', qb, kb) * (D ** -0.5)
    qpos = jnp.arange(N)[:, None] * W + jnp.arange(W)[None, :]
    kpos = jnp.arange(N)[:, None] * W - W + jnp.arange(3 * W)[None, :]
    dist = jnp.abs(qpos[:, :, None] - kpos[:, None, :])
    valid = (dist <= W) & ((kpos >= 0) & (kpos < T))[:, None, :]
    slopes = jnp.exp2(-ALIBI_MAX * jnp.arange(1, AT_HEADS + 1, dtype=f32) / AT_HEADS)
    slopes = slopes.reshape(AT_KV_HEADS, G)
    bias = -slopes[None, :, :, None, None] * dist.astype(f32)[:, None, None]
    s = jnp.where(valid[None, :, None, None], s + bias[None], -jnp.inf)
    sink_logit = jnp.broadcast_to(
        sink.astype(f32).reshape(AT_KV_HEADS, G)[None, None, :, :, None, None], s.shape[:-1] + (1,))
    p = jax.nn.softmax(jnp.concatenate([s, sink_logit], axis=-1), axis=-1)[..., :-1]
    o = jnp.einsum('bnhgqk,bnkhd->bnqhgd', p, vb)
    return o.reshape(B, T, AT_HEADS * D)


def encoder_layer(x, ffn1_norm, ffn1_w_gate, ffn1_w_up, ffn1_w_down, mix_norm, w_in,
                  dn_conv, dn_a_log, dn_dt_bias, dn_out_norm, cv_dw, cv_dw_bias, cv_ln_g, cv_ln_b,
                  at_q_norm, at_k_norm, at_sink, w_out, ffn2_norm, ffn2_w_gate, ffn2_w_up,
                  ffn2_w_down, final_norm):
    x = x + 0.5 * swiglu(rms_norm(x, ffn1_norm), ffn1_w_gate, ffn1_w_up, ffn1_w_down)
    h = rms_norm(x, mix_norm)
    proj = h @ w_in
    qkv_dn, z_dn, b_dn, a_dn, glu_in, q_at, kv_at = jnp.split(
        proj, np.cumsum(IN_SIZES)[:-1].tolist(), axis=-1)
    o_dn = deltanet_mixer(qkv_dn, z_dn, b_dn, a_dn, dn_conv, dn_a_log, dn_dt_bias, dn_out_norm)
    o_cv = conformer_conv_mixer(glu_in, cv_dw, cv_dw_bias, cv_ln_g, cv_ln_b)
    o_at = windowed_gqa_sink(q_at, kv_at, at_q_norm, at_k_norm, at_sink)
    mixed = jnp.concatenate([o_dn.astype(x.dtype), o_cv.astype(x.dtype), o_at.astype(x.dtype)], axis=-1)
    x = x + mixed @ w_out
    x = x + 0.5 * swiglu(rms_norm(x, ffn2_norm), ffn2_w_gate, ffn2_w_up, ffn2_w_down)
    return rms_norm(x, final_norm)


def setup_inputs(seed: int = 0) -> dict:
    key = jax.random.key(seed)
    ks = jax.random.split(key, 26)
    f32 = jnp.float32
    L = DEPTH

    def nrm(k, shape, scale):
        return jax.random.normal(k, shape, f32) * scale

    def gain(k, shape):
        return 1.0 + 0.02 * jax.random.normal(k, shape, f32)

    dt = jnp.exp(jax.random.uniform(ks[9], (L, 2, DN_HEADS), f32, math.log(1e-3), math.log(1e-1)))
    return {
        'x_prompt': nrm(ks[0], (BATCH, SEQ, D_MODEL), 1.0),
        'x_sample': nrm(ks[1], (DEC_BATCH, DEC_SEQ, D_MODEL), 1.0),
        'ffn1_norm': gain(ks[2], (L, D_MODEL)),
        'ffn1_w_gate': nrm(ks[3], (L, D_MODEL, D_FF), D_MODEL ** -0.5),
        'ffn1_w_up': nrm(ks[4], (L, D_MODEL, D_FF), D_MODEL ** -0.5),
        'ffn1_w_down': nrm(ks[5], (L, D_FF, D_MODEL), D_FF ** -0.5),
        'mix_norm': gain(ks[6], (L, D_MODEL)),
        'w_in': nrm(ks[7], (L, D_MODEL, IN_COLS), D_MODEL ** -0.5),
        'dn_conv': nrm(ks[8], (L, DN_CONV, 3 * DN_HEADS * DN_DK), DN_CONV ** -0.5),
        'dn_a_log': jnp.log(jax.random.uniform(ks[10], (L, 2, DN_HEADS), f32, 1.0, 16.0)),
        'dn_dt_bias': dt + jnp.log(-jnp.expm1(-dt)),
        'dn_out_norm': gain(ks[11], (L, DN_DV)),
        'cv_dw': nrm(ks[12], (L, CV_KERNEL, CV_CH), CV_KERNEL ** -0.5),
        'cv_dw_bias': nrm(ks[13], (L, CV_CH), 0.02),
        'cv_ln_g': gain(ks[14], (L, CV_CH)),
        'cv_ln_b': nrm(ks[15], (L, CV_CH), 0.02),
        'at_q_norm': gain(ks[16], (L, AT_HEAD_DIM)),
        'at_k_norm': gain(ks[17], (L, AT_HEAD_DIM)),
        'at_sink': nrm(ks[18], (L, AT_HEADS), 0.5),
        'w_out': nrm(ks[19], (L, MIX_OUT, D_MODEL), MIX_OUT ** -0.5),
        'ffn2_norm': gain(ks[20], (L, D_MODEL)),
        'ffn2_w_gate': nrm(ks[21], (L, D_MODEL, D_FF), D_MODEL ** -0.5),
        'ffn2_w_up': nrm(ks[22], (L, D_MODEL, D_FF), D_MODEL ** -0.5),
        'ffn2_w_down': nrm(ks[23], (L, D_FF, D_MODEL), D_FF ** -0.5),
        'final_norm': gain(ks[24], (L, D_MODEL)),
    }


def reference(x_prompt, x_sample, ffn1_norm, ffn1_w_gate, ffn1_w_up, ffn1_w_down, mix_norm, w_in,
              dn_conv, dn_a_log, dn_dt_bias, dn_out_norm, cv_dw, cv_dw_bias, cv_ln_g, cv_ln_b,
              at_q_norm, at_k_norm, at_sink, w_out, ffn2_norm, ffn2_w_gate, ffn2_w_up, ffn2_w_down,
              final_norm):
    params = (ffn1_norm, ffn1_w_gate, ffn1_w_up, ffn1_w_down, mix_norm, w_in,
              dn_conv, dn_a_log, dn_dt_bias, dn_out_norm, cv_dw, cv_dw_bias, cv_ln_g, cv_ln_b,
              at_q_norm, at_k_norm, at_sink, w_out, ffn2_norm, ffn2_w_gate, ffn2_w_up, ffn2_w_down,
              final_norm)

    def trunk(x):
        for l in range(DEPTH):
            x = encoder_layer(x, *[p[l] for p in params])
        return x

    y_prompt = trunk(x_prompt)
    y_sample = trunk(x_sample)
    return (y_prompt, y_sample)
```

```python
import functools
import math

import jax
import jax.numpy as jnp
from jax import lax
from jax.experimental import pallas as pl
from jax.experimental.pallas import tpu as pltpu

F32 = jnp.float32
BF16 = jnp.bfloat16
EPS = 1e-6

DN_HEADS = 4
DN_DK = 128
DN_DV = 128
DN_CONV = 5
CV_KERNEL = 31
AT_HEADS = 4
AT_KV_HEADS = 2
AT_HEAD_DIM = 64
AT_WINDOW = 128
ALIBI_MAX = 8.0

LANES = 128
BF16_SUBLANES = 16

ROW_TILE = 512
FF_CHUNK = 256
TIME_TILE = 512
DN_CHUNK = 128
CONV_ROWS = 64
VMEM_LIMIT = 56 * 1024 * 1024


def _dot(a, b):
    return jnp.dot(a, b, preferred_element_type=F32)


def _dot_nt(a, b):
    return lax.dot_general(a, b, (((1,), (1,)), ((), ())), preferred_element_type=F32)


def _dot_tn(a, b):
    return lax.dot_general(a, b, (((0,), (0,)), ((), ())), preferred_element_type=F32)


def _rms(x, g):
    return x * lax.rsqrt(jnp.mean(x * x, axis=-1, keepdims=True) + EPS) * g


def _silu(x):
    return x * jax.nn.sigmoid(x)


def _softplus(x):
    return jnp.maximum(x, 0.0) + jnp.log1p(jnp.exp(-jnp.abs(x)))


def _const_spec(shape):
    nd = len(shape)
    return pl.BlockSpec(shape, lambda *_: (0,) * nd, pipeline_mode=pl.Buffered(1))


def _swiglu(h_bf, wg_ref, wu_ref, wd_ref, act_ref):
    d_ff = wg_ref.shape[1]
    for c in range(0, d_ff, FF_CHUNK):
        gate = _dot(h_bf, wg_ref[:, c:c + FF_CHUNK])
        up = _dot(h_bf, wu_ref[:, c:c + FF_CHUNK])
        act_ref[:, c:c + FF_CHUNK] = (_silu(gate) * up).astype(BF16)
    return _dot(act_ref[...], wd_ref[...])


def _ffn_in_kernel(x_ref, g1_ref, wg_ref, wu_ref, wd_ref, gm_ref,
                   wqkv_ref, wz_ref, wglu_ref, wq_ref, wkv_ref, wab_ref,
                   x1_ref, qkv_ref, z_ref, glu_ref, q_ref, kv_ref, ab_ref,
                   act_ref):
    x = x_ref[...]
    h = _rms(x, g1_ref[...]).astype(BF16)
    x1 = x + 0.5 * _swiglu(h, wg_ref, wu_ref, wd_ref, act_ref)
    x1_ref[...] = x1
    h2 = _rms(x1, gm_ref[...]).astype(BF16)
    qkv_ref[...] = _dot(h2, wqkv_ref[...]).astype(BF16)
    z_ref[...] = _dot(h2, wz_ref[...]).astype(BF16)
    glu_ref[...] = _dot(h2, wglu_ref[...]).astype(BF16)
    q_ref[...] = _dot(h2, wq_ref[...]).astype(BF16)
    kv_ref[...] = _dot(h2, wkv_ref[...]).astype(BF16)
    ab_ref[...] = _dot(h2, wab_ref[...])


def _ffn_in_call(x, p):
    rows, d = x.shape
    tm = min(ROW_TILE, rows)
    d_ff = p['wg1'].shape[1]
    weights = [p['g1'], p['wg1'], p['wu1'], p['wd1'], p['gm'],
               p['w_qkv'], p['w_z'], p['w_glu'], p['w_q'], p['w_kv'], p['w_ab']]
    out_cols = [(d, F32)] + [(w.shape[1], BF16) for w in weights[5:10]] + [(LANES, F32)]
    row_spec = lambda n: pl.BlockSpec((tm, n), lambda i: (i, 0))
    return pl.pallas_call(
        _ffn_in_kernel,
        grid=(rows // tm,),
        in_specs=[row_spec(d)] + [_const_spec(w.shape) for w in weights],
        out_specs=[row_spec(n) for n, _ in out_cols],
        out_shape=[jax.ShapeDtypeStruct((rows, n), dt) for n, dt in out_cols],
        scratch_shapes=[pltpu.VMEM((tm, d_ff), BF16)],
        compiler_params=pltpu.CompilerParams(
            dimension_semantics=("parallel",), vmem_limit_bytes=VMEM_LIMIT),
        name="ffn_in",
    )(x, *weights)


def _out_ffn_kernel(x1_ref, of_ref, ob_ref, z_ref, ocv_ref, oat_ref, og_ref, wo_ref,
                    g2_ref, wg_ref, wu_ref, wd_ref, gf_ref, y_ref, mix_ref, act_ref):
    dn_cols = of_ref.shape[1]
    for h in range(dn_cols // DN_DV):
        sl = slice(h * DN_DV, (h + 1) * DN_DV)
        o = _rms(of_ref[:, sl] + ob_ref[:, sl], og_ref[...])
        mix_ref[:, sl] = (o * _silu(z_ref[:, sl].astype(F32))).astype(BF16)
    cv_cols = ocv_ref.shape[1]
    mix_ref[:, dn_cols:dn_cols + cv_cols] = ocv_ref[...]
    mix_ref[:, dn_cols + cv_cols:] = oat_ref[...]
    x2 = x1_ref[...] + _dot(mix_ref[...], wo_ref[...])
    h2 = _rms(x2, g2_ref[...]).astype(BF16)
    x3 = x2 + 0.5 * _swiglu(h2, wg_ref, wu_ref, wd_ref, act_ref)
    y_ref[...] = _rms(x3, gf_ref[...])


def _out_ffn_call(x1, o_f, o_b, z, o_cv, o_at, p):
    rows, d = x1.shape
    tm = min(ROW_TILE, rows)
    d_ff = p['wg2'].shape[1]
    weights = [p['og'], p['w_out'], p['g2'], p['wg2'], p['wu2'], p['wd2'], p['gf']]
    acts = [x1, o_f, o_b, z, o_cv, o_at]
    row_spec = lambda n: pl.BlockSpec((tm, n), lambda i: (i, 0))
    return pl.pallas_call(
        _out_ffn_kernel,
        grid=(rows // tm,),
        in_specs=[row_spec(a.shape[1]) for a in acts] + [_const_spec(w.shape) for w in weights],
        out_specs=row_spec(d),
        out_shape=jax.ShapeDtypeStruct((rows, d), F32),
        scratch_shapes=[pltpu.VMEM((tm, d), BF16), pltpu.VMEM((tm, d_ff), BF16)],
        compiler_params=pltpu.CompilerParams(
            dimension_semantics=("parallel",), vmem_limit_bytes=VMEM_LIMIT),
        name="out_ffn",
    )(*acts, *weights)


def _halo_specs(tt, halo, t_len, cols):
    per = tt // halo
    last = t_len // halo - 1
    prev = pl.BlockSpec((None, halo, cols), lambda b, i: (b, jnp.maximum(i * per - 1, 0), 0))
    cur = pl.BlockSpec((None, tt, cols), lambda b, i: (b, i, 0))
    nxt = pl.BlockSpec((None, halo, cols), lambda b, i: (b, jnp.minimum((i + 1) * per, last), 0))
    return [prev, cur, nxt]


def _fill_with_halo(dst_ref, prev, cur, nxt, halo, tt):
    i = pl.program_id(1)
    last = pl.num_programs(1) - 1
    dst_ref[0:halo, :] = jnp.where(i > 0, prev, 0.0)
    dst_ref[halo:halo + tt, :] = cur
    dst_ref[halo + tt:halo + tt + halo, :] = jnp.where(i < last, nxt, 0.0)


def _dn_prep_kernel(prev_ref, cur_ref, next_ref, w_ref, out_ref, xf_ref, *, tt, halo):
    _fill_with_halo(xf_ref, prev_ref[...].astype(F32), cur_ref[...].astype(F32),
                    next_ref[...].astype(F32), halo, tt)
    cols = cur_ref.shape[1]
    left = (DN_CONV - 1) // 2
    qk_groups = 2 * DN_HEADS * DN_DK // LANES
    q_groups = DN_HEADS * DN_DK // LANES
    for rb in range(tt // CONV_ROWS):
        r0 = rb * CONV_ROWS
        for j in range(cols // LANES):
            ls = slice(j * LANES, (j + 1) * LANES)
            win = xf_ref[r0:r0 + CONV_ROWS + 2 * halo, ls]
            acc = None
            for k in range(DN_CONV):
                off = halo - left + k
                term = win[off:off + CONV_ROWS, :] * w_ref[k:k + 1, ls]
                acc = term if acc is None else acc + term
            y = _silu(acc)
            if j < qk_groups:
                y = y * lax.rsqrt(jnp.sum(y * y, axis=-1, keepdims=True) + EPS)
            if j < q_groups:
                y = y * (DN_DK ** -0.5)
            out_ref[r0:r0 + CONV_ROWS, ls] = y.astype(BF16)


def _dn_prep_call(qkv, conv_w):
    b, t_len, cols = qkv.shape
    tt = min(TIME_TILE, t_len)
    halo = BF16_SUBLANES
    return pl.pallas_call(
        functools.partial(_dn_prep_kernel, tt=tt, halo=halo),
        grid=(b, t_len // tt),
        in_specs=_halo_specs(tt, halo, t_len, cols) + [pl.BlockSpec(conv_w.shape, lambda b, i: (0, 0))],
        out_specs=pl.BlockSpec((None, tt, cols), lambda b, i: (b, i, 0)),
        out_shape=jax.ShapeDtypeStruct((b, t_len, cols), BF16),
        scratch_shapes=[pltpu.VMEM((tt + 2 * halo, cols), F32)],
        compiler_params=pltpu.CompilerParams(
            dimension_semantics=("parallel", "parallel"), vmem_limit_bytes=VMEM_LIMIT),
        name="dn_prep",
    )(qkv, qkv, qkv, conv_w)


def _split_dot(m_bf, x):
    hi = x.astype(BF16)
    r1 = x - hi.astype(F32)
    mid = r1.astype(BF16)
    lo = (r1 - mid.astype(F32)).astype(BF16)
    return _dot(m_bf, hi) + _dot(m_bf, mid) + _dot(m_bf, lo)


def _dn_scan_kernel(qf_ref, qb_ref, abf_ref, abb_ref, pv_ref, of_ref, ob_ref, s_ref, *, tt, c):
    @pl.when(pl.program_id(1) == 0)
    def _():
        s_ref[...] = jnp.zeros_like(s_ref)

    nchunk = tt // c
    row = lax.broadcasted_iota(jnp.int32, (c, c), 0)
    col = lax.broadcasted_iota(jnp.int32, (c, c), 1)
    incl = (row >= col, row <= col)
    strict = (row > col, row < col)
    tri_bf = (incl[0].astype(BF16), incl[1].astype(BF16))
    dirs = ((qf_ref, abf_ref, of_ref), (qb_ref, abb_ref, ob_ref))
    hk = DN_HEADS * DN_DK
    gate_lane = 2 * DN_HEADS
    n_double = int(math.log2(c)) - 1

    def chunk_body(n, carry):
        for d in range(2):
            x_ref, ab_ref, o_ref = dirs[d]
            nn = n if d == 0 else nchunk - 1 - n
            r0 = pl.multiple_of(nn * c, c)
            ab = ab_ref[pl.ds(r0, c), :]
            beta = jax.nn.sigmoid(ab)
            g = -jnp.exp(pv_ref[0:1, :]) * _softplus(ab + pv_ref[1:2, :])
            gc = _split_dot(tri_bf[d], g)
            gc_t = gc.T
            tot = gc[c - 1:c, :] if d == 0 else gc[0:1, :]
            e_gc = jnp.exp(gc)
            e_kd = jnp.exp(tot - gc)
            e_tot = jnp.exp(tot)
            for h in range(DN_HEADS):
                j = d * DN_HEADS + h
                gj = gate_lane + j
                q = x_ref[pl.ds(r0, c), h * DN_DK:(h + 1) * DN_DK]
                k = x_ref[pl.ds(r0, c), hk + h * DN_DK:hk + (h + 1) * DN_DK]
                v = x_ref[pl.ds(r0, c), 2 * hk + h * DN_DV:2 * hk + (h + 1) * DN_DV]
                kf = k.astype(F32)
                b_col = beta[:, j:j + 1]
                kb = kf * b_col
                diff = gc[:, gj:gj + 1] - gc_t[gj:gj + 1, :]
                decay = jnp.exp(jnp.where(incl[d], diff, -jnp.inf))
                gq = _dot_nt(jnp.concatenate([kb.astype(BF16), q], axis=0), k)
                nm = jnp.where(strict[d], -(gq[:c] * decay), 0.0)
                r = nm
                m = nm
                for _ in range(n_double):
                    m_bf = m.astype(BF16)
                    m = _dot(m_bf, m_bf)
                    r = r + m + _dot(r.astype(BF16), m.astype(BF16))
                qk = gq[c:] * decay
                eg_col = e_gc[:, gj:gj + 1]
                xs = jnp.concatenate([v.astype(F32) * b_col, kb * eg_col], axis=1)
                uw = xs + _dot(r.astype(BF16), xs.astype(BF16))
                u = uw[:, :DN_DV]
                w = uw[:, DN_DV:]
                qd = q.astype(F32) * eg_col
                kd = kf * e_kd[:, gj:gj + 1]
                s = s_ref[j]
                wq = _dot(jnp.concatenate([w.astype(BF16), qd.astype(BF16)], axis=0), s.astype(BF16))
                vn = (u - wq[:c]).astype(BF16)
                o = wq[c:] + _dot(qk.astype(BF16), vn)
                s_ref[j] = s * e_tot[:, gj:gj + 1] + _dot_tn(kd.astype(BF16), vn)
                o_ref[pl.ds(r0, c), h * DN_DV:(h + 1) * DN_DV] = o
        return carry

    lax.fori_loop(0, nchunk, chunk_body, 0)


def _dn_scan_call(qkvn, ab, pvec):
    b, t_len, cols = qkvn.shape
    tt = min(TIME_TILE, t_len)
    nt = t_len // tt
    c = min(DN_CHUNK, tt)
    dv = DN_HEADS * DN_DV
    fwd = lambda n: pl.BlockSpec((None, tt, n), lambda b, i: (b, i, 0))
    bwd = lambda n: pl.BlockSpec((None, tt, n), lambda b, i: (b, nt - 1 - i, 0))
    return pl.pallas_call(
        functools.partial(_dn_scan_kernel, tt=tt, c=c),
        grid=(b, nt),
        in_specs=[fwd(cols), bwd(cols), fwd(LANES), bwd(LANES),
                  pl.BlockSpec(pvec.shape, lambda b, i: (0, 0))],
        out_specs=[fwd(dv), bwd(dv)],
        out_shape=[jax.ShapeDtypeStruct((b, t_len, dv), F32)] * 2,
        scratch_shapes=[pltpu.VMEM((2 * DN_HEADS, DN_DK, DN_DV), F32)],
        compiler_params=pltpu.CompilerParams(
            dimension_semantics=("parallel", "arbitrary"), vmem_limit_bytes=VMEM_LIMIT),
        name="dn_scan",
    )(qkvn, qkvn, ab, ab, pvec)


def _conf_kernel(prev_ref, cur_ref, next_ref, dw_ref, b_ref, g_ref, bb_ref, out_ref, hf_ref,
                 *, tt, halo):
    ch = out_ref.shape[1]

    def glu(x_ref):
        x = x_ref[...].astype(F32)
        return x[:, :ch] * jax.nn.sigmoid(x[:, ch:])

    _fill_with_halo(hf_ref, glu(prev_ref), glu(cur_ref), glu(next_ref), halo, tt)
    left = (CV_KERNEL - 1) // 2
    for rb in range(tt // CONV_ROWS):
        r0 = rb * CONV_ROWS
        win = hf_ref[r0:r0 + CONV_ROWS + 2 * halo, :]
        acc = None
        for k in range(CV_KERNEL):
            off = halo - left + k
            term = win[off:off + CONV_ROWS, :] * dw_ref[k:k + 1, :]
            acc = term if acc is None else acc + term
        acc = acc + b_ref[...]
        xc = acc - jnp.mean(acc, axis=-1, keepdims=True)
        var = jnp.mean(xc * xc, axis=-1, keepdims=True)
        y = xc * lax.rsqrt(var + EPS) * g_ref[...] + bb_ref[...]
        out_ref[r0:r0 + CONV_ROWS, :] = _silu(y).astype(BF16)


def _conf_call(glu_in, dw, dw_b, ln_g, ln_b):
    b, t_len, cols = glu_in.shape
    ch = cols // 2
    tt = min(TIME_TILE, t_len)
    halo = BF16_SUBLANES
    small = [dw, dw_b, ln_g, ln_b]
    return pl.pallas_call(
        functools.partial(_conf_kernel, tt=tt, halo=halo),
        grid=(b, t_len // tt),
        in_specs=_halo_specs(tt, halo, t_len, cols)
        + [pl.BlockSpec(a.shape, lambda b, i: (0, 0)) for a in small],
        out_specs=pl.BlockSpec((None, tt, ch), lambda b, i: (b, i, 0)),
        out_shape=jax.ShapeDtypeStruct((b, t_len, ch), BF16),
        scratch_shapes=[pltpu.VMEM((tt + 2 * halo, ch), F32)],
        compiler_params=pltpu.CompilerParams(
            dimension_semantics=("parallel", "parallel"), vmem_limit_bytes=VMEM_LIMIT),
        name="conformer",
    )(glu_in, glu_in, glu_in, *small)


def _attn_kernel(q_ref, kp_ref, kc_ref, kn_ref, qg_ref, kg_ref, sink_ref, out_ref,
                 kd_ref, vd_ref, bias_ref, *, tq, w):
    i = pl.program_id(1)
    nblk = pl.num_programs(1) * (tq // w)
    d = AT_HEAD_DIM
    lane = lax.broadcasted_iota(jnp.int32, (1, 2 * d), 1)
    lo = lane < d

    @pl.when(i == 0)
    def _():
        qi = lax.broadcasted_iota(jnp.int32, (w, 3 * w), 0)
        kj = lax.broadcasted_iota(jnp.int32, (w, 3 * w), 1)
        dist = jnp.abs(qi - (kj - w))
        for h in range(AT_HEADS):
            slope = 2.0 ** (-ALIBI_MAX * (h + 1) / AT_HEADS)
            bias_ref[h] = jnp.where(dist <= w, -slope * dist.astype(F32), -jnp.inf)

    def head_rms(x, g):
        sq = x * x
        s_lo = jnp.sum(jnp.where(lo, sq, 0.0), axis=-1, keepdims=True)
        s_hi = jnp.sum(jnp.where(lo, 0.0, sq), axis=-1, keepdims=True)
        ms = jnp.where(lo, s_lo, s_hi) * (1.0 / d)
        return x * lax.rsqrt(ms + EPS) * g

    def fill_kv(src_ref, r0, rows):
        x = src_ref[...]
        kn = head_rms(x[:, :2 * d].astype(F32), kg_ref[...])
        vf = x[:, 2 * d:].astype(F32)
        k_roll = pltpu.roll(kn, d, axis=1)
        v_roll = pltpu.roll(vf, d, axis=1)
        kd_ref[0, r0:r0 + rows, :] = jnp.where(lo, kn, k_roll).astype(BF16)
        kd_ref[1, r0:r0 + rows, :] = jnp.where(lo, k_roll, kn).astype(BF16)
        vd_ref[0, r0:r0 + rows, :] = jnp.where(lo, vf, v_roll).astype(BF16)
        vd_ref[1, r0:r0 + rows, :] = jnp.where(lo, v_roll, vf).astype(BF16)

    fill_kv(kp_ref, 0, w)
    fill_kv(kc_ref, w, tq)
    fill_kv(kn_ref, w + tq, w)

    key_blk = lax.broadcasted_iota(jnp.int32, (1, 3 * w), 1) // w
    group = AT_HEADS // AT_KV_HEADS
    for jb in range(tq // w):
        n = i * (tq // w) + jb
        dead = ((key_blk == 0) & (n == 0)) | ((key_blk == 2) & (n == nblk - 1))
        edge = jnp.where(dead, -jnp.inf, 0.0)
        for g in range(AT_KV_HEADS):
            qn = head_rms(q_ref[jb * w:(jb + 1) * w, g * 2 * d:(g + 1) * 2 * d].astype(F32),
                          qg_ref[...]) * (d ** -0.5)
            lhs = jnp.concatenate([jnp.where(lo, qn, 0.0), jnp.where(lo, 0.0, qn)], axis=0)
            s = _dot_nt(lhs.astype(BF16), kd_ref[g, jb * w:jb * w + 3 * w, :])
            ps = []
            inv = []
            for hh in range(group):
                h = g * group + hh
                sink = sink_ref[h]
                sh = s[hh * w:(hh + 1) * w] + bias_ref[h] + edge
                m = jnp.maximum(jnp.max(sh, axis=-1, keepdims=True), sink)
                p = jnp.exp(sh - m)
                den = jnp.sum(p, axis=-1, keepdims=True) + jnp.exp(sink - m)
                ps.append(p.astype(BF16))
                inv.append(1.0 / den)
            pv = _dot(jnp.concatenate(ps, axis=0), vd_ref[g, jb * w:jb * w + 3 * w, :])
            o = jnp.where(lo, pv[:w] * inv[0], pv[w:] * inv[1])
            out_ref[jb * w:(jb + 1) * w, g * 2 * d:(g + 1) * 2 * d] = o.astype(BF16)


def _attn_call(q, kv, q_g2, k_g2, sink):
    b, t_len, qc = q.shape
    w = AT_WINDOW
    tq = min(TIME_TILE, t_len)
    kvc = kv.shape[2]
    two_d = 2 * AT_HEAD_DIM
    return pl.pallas_call(
        functools.partial(_attn_kernel, tq=tq, w=w),
        grid=(b, t_len // tq),
        in_specs=[pl.BlockSpec((None, tq, qc), lambda b, i: (b, i, 0))]
        + _halo_specs(tq, w, t_len, kvc)
        + [pl.BlockSpec(q_g2.shape, lambda b, i: (0, 0)),
           pl.BlockSpec(k_g2.shape, lambda b, i: (0, 0)),
           pl.BlockSpec(memory_space=pltpu.SMEM)],
        out_specs=pl.BlockSpec((None, tq, qc), lambda b, i: (b, i, 0)),
        out_shape=jax.ShapeDtypeStruct((b, t_len, qc), BF16),
        scratch_shapes=[pltpu.VMEM((AT_KV_HEADS, tq + 2 * w, two_d), BF16),
                        pltpu.VMEM((AT_KV_HEADS, tq + 2 * w, two_d), BF16),
                        pltpu.VMEM((AT_HEADS, w, 3 * w), F32)],
        compiler_params=pltpu.CompilerParams(
            dimension_semantics=("parallel", "arbitrary"), vmem_limit_bytes=VMEM_LIMIT),
        name="band_attn",
    )(q, kv, kv, kv, q_g2, k_g2, sink)


def _layer_params(l, ffn1_norm, ffn1_w_gate, ffn1_w_up, ffn1_w_down, mix_norm, w_in,
                  dn_conv, dn_a_log, dn_dt_bias, dn_out_norm, cv_dw, cv_dw_bias, cv_ln_g, cv_ln_b,
                  at_q_norm, at_k_norm, at_sink, w_out, ffn2_norm, ffn2_w_gate, ffn2_w_up,
                  ffn2_w_down, final_norm):
    row = lambda a: a[l].reshape(1, -1).astype(F32)
    bf = lambda a: a.astype(BF16)
    hk = DN_HEADS * DN_DK
    hv = DN_HEADS * DN_DV
    cv_ch = cv_dw.shape[2]
    sizes = (2 * hk + hv, hv, 2 * DN_HEADS, 2 * DN_HEADS, 2 * cv_ch,
             AT_HEADS * AT_HEAD_DIM, 2 * AT_KV_HEADS * AT_HEAD_DIM)
    offs = [0]
    for s in sizes:
        offs.append(offs[-1] + s)
    wi = w_in[l]
    piece = lambda n: wi[:, offs[n]:offs[n + 1]]
    n_gate = 4 * DN_HEADS
    w_ab = jnp.pad(jnp.concatenate([piece(2), piece(3)], axis=1), ((0, 0), (0, LANES - n_gate)))
    pad_gate = lambda a: jnp.pad(a.reshape(1, -1).astype(F32),
                                 ((0, 0), (2 * DN_HEADS, LANES - n_gate)))
    pvec = jnp.concatenate([pad_gate(dn_a_log[l]), pad_gate(dn_dt_bias[l])], axis=0)
    two = lambda a: jnp.concatenate([row(a), row(a)], axis=1)
    return dict(
        g1=row(ffn1_norm), wg1=bf(ffn1_w_gate[l]), wu1=bf(ffn1_w_up[l]), wd1=bf(ffn1_w_down[l]),
        gm=row(mix_norm), w_qkv=bf(piece(0)), w_z=bf(piece(1)), w_ab=bf(w_ab), w_glu=bf(piece(4)),
        w_q=bf(piece(5)), w_kv=bf(piece(6)),
        dn_conv=dn_conv[l].astype(F32), pvec=pvec, og=row(dn_out_norm),
        cv_dw=cv_dw[l].astype(F32), cv_b=row(cv_dw_bias), cv_g=row(cv_ln_g), cv_bb=row(cv_ln_b),
        q_g2=two(at_q_norm), k_g2=two(at_k_norm), sink=at_sink[l].astype(F32),
        w_out=bf(w_out[l]), g2=row(ffn2_norm), wg2=bf(ffn2_w_gate[l]), wu2=bf(ffn2_w_up[l]),
        wd2=bf(ffn2_w_down[l]), gf=row(final_norm))


def _layer(x, b, t_len, p):
    x1, qkv, z, glu, q_at, kv_at, ab = _ffn_in_call(x, p)
    seq = lambda a: a.reshape(b, t_len, a.shape[-1])
    flat = lambda a: a.reshape(b * t_len, a.shape[-1])
    qkvn = _dn_prep_call(seq(qkv), p['dn_conv'])
    o_f, o_b = _dn_scan_call(qkvn, seq(ab), p['pvec'])
    o_cv = _conf_call(seq(glu), p['cv_dw'], p['cv_b'], p['cv_g'], p['cv_bb'])
    o_at = _attn_call(seq(q_at), seq(kv_at), p['q_g2'], p['k_g2'], p['sink'])
    return _out_ffn_call(x1, flat(o_f), flat(o_b), z, flat(o_cv), flat(o_at), p)


def kernel(x_prompt, x_sample, ffn1_norm, ffn1_w_gate, ffn1_w_up, ffn1_w_down, mix_norm, w_in, dn_conv, dn_a_log, dn_dt_bias, dn_out_norm, cv_dw, cv_dw_bias, cv_ln_g, cv_ln_b, at_q_norm, at_k_norm, at_sink, w_out, ffn2_norm, ffn2_w_gate, ffn2_w_up, ffn2_w_down, final_norm):
    params = (ffn1_norm, ffn1_w_gate, ffn1_w_up, ffn1_w_down, mix_norm, w_in,
              dn_conv, dn_a_log, dn_dt_bias, dn_out_norm, cv_dw, cv_dw_bias, cv_ln_g, cv_ln_b,
              at_q_norm, at_k_norm, at_sink, w_out, ffn2_norm, ffn2_w_gate, ffn2_w_up, ffn2_w_down,
              final_norm)
    depth = ffn1_norm.shape[0]
    layers = [_layer_params(l, *params) for l in range(depth)]

    def trunk(x):
        b, t_len, d = x.shape
        y = x.reshape(b * t_len, d)
        for p in layers:
            y = _layer(y, b, t_len, p)
        return y.reshape(b, t_len, d)

    return (trunk(x_prompt), trunk(x_sample))
```

```python
import functools
import math

import jax
import jax.numpy as jnp
from jax import lax
from jax.experimental import pallas as pl
from jax.experimental.pallas import tpu as pltpu

F32 = jnp.float32
BF16 = jnp.bfloat16
EPS = 1e-6

DN_HEADS = 4
DN_DK = 128
DN_DV = 128
DN_CONV = 5
CV_KERNEL = 31
AT_HEADS = 4
AT_KV_HEADS = 2
AT_HEAD_DIM = 64
AT_WINDOW = 128
ALIBI_MAX = 8.0

LANES = 128
BF16_SUBLANES = 16
F32_SUBLANES = 8

ROW_TILE = 512
FF_CHUNK = 256
TIME_TILE = 512
DN_CHUNK = 128
CONV_ROWS = 64
VMEM_LIMIT = 56 * 1024 * 1024


def _dot(a, b):
    return jnp.dot(a, b, preferred_element_type=F32)


def _dot_nt(a, b):
    return lax.dot_general(a, b, (((1,), (1,)), ((), ())), preferred_element_type=F32)


def _dot_tn(a, b):
    return lax.dot_general(a, b, (((0,), (0,)), ((), ())), preferred_element_type=F32)


def _rms(x, g):
    return x * lax.rsqrt(jnp.mean(x * x, axis=-1, keepdims=True) + EPS) * g


def _silu(x):
    return x * jax.nn.sigmoid(x)


def _softplus(x):
    return jnp.maximum(x, 0.0) + jnp.log1p(jnp.exp(-jnp.abs(x)))


def _const_spec(shape):
    nd = len(shape)
    return pl.BlockSpec(shape, lambda *_: (0,) * nd, pipeline_mode=pl.Buffered(1))


def _swiglu(h_bf, wg_ref, wu_ref, wd_ref, act_ref):
    d_ff = wg_ref.shape[1]
    for c in range(0, d_ff, FF_CHUNK):
        gate = _dot(h_bf, wg_ref[:, c:c + FF_CHUNK])
        up = _dot(h_bf, wu_ref[:, c:c + FF_CHUNK])
        act_ref[:, c:c + FF_CHUNK] = (_silu(gate) * up).astype(BF16)
    return _dot(act_ref[...], wd_ref[...])


def _ffn_in_kernel(x_ref, g1_ref, wg_ref, wu_ref, wd_ref, gm_ref,
                   wqkv_ref, wz_ref, wglu_ref, wq_ref, wkv_ref, wab_ref,
                   x1_ref, qkv_ref, z_ref, glu_ref, q_ref, kv_ref, ab_ref,
                   act_ref):
    x = x_ref[...]
    h = _rms(x, g1_ref[...]).astype(BF16)
    x1 = x + 0.5 * _swiglu(h, wg_ref, wu_ref, wd_ref, act_ref)
    x1_ref[...] = x1
    h2 = _rms(x1, gm_ref[...]).astype(BF16)
    qkv_ref[...] = _dot(h2, wqkv_ref[...]).astype(BF16)
    z_ref[...] = _dot(h2, wz_ref[...]).astype(BF16)
    glu_ref[...] = _dot(h2, wglu_ref[...]).astype(BF16)
    q_ref[...] = _dot(h2, wq_ref[...]).astype(BF16)
    kv_ref[...] = _dot(h2, wkv_ref[...]).astype(BF16)
    ab_ref[...] = _dot(h2, wab_ref[...])


def _ffn_in_call(x, p):
    rows, d = x.shape
    tm = min(ROW_TILE, rows)
    d_ff = p['wg1'].shape[1]
    weights = [p['g1'], p['wg1'], p['wu1'], p['wd1'], p['gm'],
               p['w_qkv'], p['w_z'], p['w_glu'], p['w_q'], p['w_kv'], p['w_ab']]
    out_cols = [(d, F32)] + [(w.shape[1], BF16) for w in weights[5:10]] + [(LANES, F32)]
    row_spec = lambda n: pl.BlockSpec((tm, n), lambda i: (i, 0))
    return pl.pallas_call(
        _ffn_in_kernel,
        grid=(rows // tm,),
        in_specs=[row_spec(d)] + [_const_spec(w.shape) for w in weights],
        out_specs=[row_spec(n) for n, _ in out_cols],
        out_shape=[jax.ShapeDtypeStruct((rows, n), dt) for n, dt in out_cols],
        scratch_shapes=[pltpu.VMEM((tm, d_ff), BF16)],
        compiler_params=pltpu.CompilerParams(
            dimension_semantics=("parallel",), vmem_limit_bytes=VMEM_LIMIT),
        name="ffn_in",
    )(x, *weights)


def _out_ffn_kernel(x1_ref, of_ref, ob_ref, z_ref, ocv_ref, oat_ref, og_ref, wo_ref,
                    g2_ref, wg_ref, wu_ref, wd_ref, gf_ref, y_ref, mix_ref, act_ref):
    dn_cols = of_ref.shape[1]
    for h in range(dn_cols // DN_DV):
        sl = slice(h * DN_DV, (h + 1) * DN_DV)
        o = _rms(of_ref[:, sl] + ob_ref[:, sl], og_ref[...])
        mix_ref[:, sl] = (o * _silu(z_ref[:, sl].astype(F32))).astype(BF16)
    cv_cols = ocv_ref.shape[1]
    mix_ref[:, dn_cols:dn_cols + cv_cols] = ocv_ref[...]
    mix_ref[:, dn_cols + cv_cols:] = oat_ref[...]
    x2 = x1_ref[...] + _dot(mix_ref[...], wo_ref[...])
    h2 = _rms(x2, g2_ref[...]).astype(BF16)
    x3 = x2 + 0.5 * _swiglu(h2, wg_ref, wu_ref, wd_ref, act_ref)
    y_ref[...] = _rms(x3, gf_ref[...])


def _out_ffn_call(x1, o_f, o_b, z, o_cv, o_at, p):
    rows, d = x1.shape
    tm = min(ROW_TILE, rows)
    d_ff = p['wg2'].shape[1]
    weights = [p['og'], p['w_out'], p['g2'], p['wg2'], p['wu2'], p['wd2'], p['gf']]
    acts = [x1, o_f, o_b, z, o_cv, o_at]
    row_spec = lambda n: pl.BlockSpec((tm, n), lambda i: (i, 0))
    return pl.pallas_call(
        _out_ffn_kernel,
        grid=(rows // tm,),
        in_specs=[row_spec(a.shape[1]) for a in acts] + [_const_spec(w.shape) for w in weights],
        out_specs=row_spec(d),
        out_shape=jax.ShapeDtypeStruct((rows, d), F32),
        scratch_shapes=[pltpu.VMEM((tm, d), BF16), pltpu.VMEM((tm, d_ff), BF16)],
        compiler_params=pltpu.CompilerParams(
            dimension_semantics=("parallel",), vmem_limit_bytes=VMEM_LIMIT),
        name="out_ffn",
    )(*acts, *weights)


def _halo_specs(tt, halo, t_len, cols):
    per = tt // halo
    last = t_len // halo - 1
    prev = pl.BlockSpec((None, halo, cols), lambda b, i: (b, jnp.maximum(i * per - 1, 0), 0))
    cur = pl.BlockSpec((None, tt, cols), lambda b, i: (b, i, 0))
    nxt = pl.BlockSpec((None, halo, cols), lambda b, i: (b, jnp.minimum((i + 1) * per, last), 0))
    return [prev, cur, nxt]


def _fill_with_halo(dst_ref, prev, cur, nxt, halo, tt):
    i = pl.program_id(1)
    last = pl.num_programs(1) - 1
    dst_ref[0:halo, :] = jnp.where(i > 0, prev, 0.0)
    dst_ref[halo:halo + tt, :] = cur
    dst_ref[halo + tt:halo + tt + halo, :] = jnp.where(i < last, nxt, 0.0)


def _dn_prep_kernel(prev_ref, cur_ref, next_ref, w_ref, out_ref, xf_ref, *, tt, halo):
    _fill_with_halo(xf_ref, prev_ref[...].astype(F32), cur_ref[...].astype(F32),
                    next_ref[...].astype(F32), halo, tt)
    cols = cur_ref.shape[1]
    left = (DN_CONV - 1) // 2
    qk_groups = 2 * DN_HEADS * DN_DK // LANES
    q_groups = DN_HEADS * DN_DK // LANES
    for rb in range(tt // CONV_ROWS):
        r0 = rb * CONV_ROWS
        for j in range(cols // LANES):
            ls = slice(j * LANES, (j + 1) * LANES)
            win = xf_ref[r0:r0 + CONV_ROWS + 2 * halo, ls]
            acc = None
            for k in range(DN_CONV):
                off = halo - left + k
                term = win[off:off + CONV_ROWS, :] * w_ref[k:k + 1, ls]
                acc = term if acc is None else acc + term
            y = _silu(acc)
            if j < qk_groups:
                y = y * lax.rsqrt(jnp.sum(y * y, axis=-1, keepdims=True) + EPS)
            if j < q_groups:
                y = y * (DN_DK ** -0.5)
            out_ref[r0:r0 + CONV_ROWS, ls] = y.astype(BF16)


def _dn_prep_call(qkv, conv_w):
    b, t_len, cols = qkv.shape
    tt = min(TIME_TILE, t_len)
    halo = BF16_SUBLANES
    return pl.pallas_call(
        functools.partial(_dn_prep_kernel, tt=tt, halo=halo),
        grid=(b, t_len // tt),
        in_specs=_halo_specs(tt, halo, t_len, cols) + [pl.BlockSpec(conv_w.shape, lambda b, i: (0, 0))],
        out_specs=pl.BlockSpec((None, tt, cols), lambda b, i: (b, i, 0)),
        out_shape=jax.ShapeDtypeStruct((b, t_len, cols), BF16),
        scratch_shapes=[pltpu.VMEM((tt + 2 * halo, cols), F32)],
        compiler_params=pltpu.CompilerParams(
            dimension_semantics=("parallel", "parallel"), vmem_limit_bytes=VMEM_LIMIT),
        name="dn_prep",
    )(qkv, qkv, qkv, conv_w)


def _split_dot(m_bf, x):
    hi = x.astype(BF16)
    r1 = x - hi.astype(F32)
    mid = r1.astype(BF16)
    lo = (r1 - mid.astype(F32)).astype(BF16)
    return _dot(m_bf, hi) + _dot(m_bf, mid) + _dot(m_bf, lo)


def _dn_scan_kernel(qf_ref, qb_ref, abf_ref, abb_ref, pv_ref, of_ref, ob_ref, s_ref, *, tt, c):
    @pl.when(pl.program_id(1) == 0)
    def _():
        s_ref[...] = jnp.zeros_like(s_ref)

    nchunk = tt // c
    row = lax.broadcasted_iota(jnp.int32, (c, c), 0)
    col = lax.broadcasted_iota(jnp.int32, (c, c), 1)
    incl = (row >= col, row <= col)
    strict = (row > col, row < col)
    tri_bf = (incl[0].astype(BF16), incl[1].astype(BF16))
    dirs = ((qf_ref, abf_ref, of_ref), (qb_ref, abb_ref, ob_ref))
    hk = DN_HEADS * DN_DK
    gate_lane = 2 * DN_HEADS
    n_double = int(math.log2(c)) - 1

    units = [(d, h) for d in range(2) for h in range(DN_HEADS)]

    def chunk_body(n, carry):
        gates = []
        for d in range(2):
            nn = n if d == 0 else nchunk - 1 - n
            r0 = pl.multiple_of(nn * c, c)
            ab = dirs[d][1][pl.ds(r0, c), :]
            beta = jax.nn.sigmoid(ab)
            g = -jnp.exp(pv_ref[0:1, :]) * _softplus(ab + pv_ref[1:2, :])
            gc = _split_dot(tri_bf[d], g)
            tot = gc[c - 1:c, :] if d == 0 else gc[0:1, :]
            gates.append(dict(r0=r0, beta=beta, gc=gc, gc_t=gc.T, e_gc=jnp.exp(gc),
                              e_kd=jnp.exp(tot - gc), e_tot=jnp.exp(tot)))

        st = []
        for d, h in units:
            ga = gates[d]
            x_ref = dirs[d][0]
            j = d * DN_HEADS + h
            gj = gate_lane + j
            rows = pl.ds(ga['r0'], c)
            q = x_ref[rows, h * DN_DK:(h + 1) * DN_DK]
            k = x_ref[rows, hk + h * DN_DK:hk + (h + 1) * DN_DK]
            v = x_ref[rows, 2 * hk + h * DN_DV:2 * hk + (h + 1) * DN_DV]
            kf = k.astype(F32)
            b_col = ga['beta'][:, j:j + 1]
            kb = kf * b_col
            eg_col = ga['e_gc'][:, gj:gj + 1]
            diff = ga['gc'][:, gj:gj + 1] - ga['gc_t'][gj:gj + 1, :]
            decay = jnp.exp(jnp.where(incl[d], diff, -jnp.inf))
            gq = _dot_nt(jnp.concatenate([kb.astype(BF16), q], axis=0), k)
            st.append(dict(
                d=d, h=h, j=j, rows=rows,
                nm=jnp.where(strict[d], -(gq[:c] * decay), 0.0),
                qk=(gq[c:] * decay).astype(BF16),
                xs=jnp.concatenate([v.astype(F32) * b_col, kb * eg_col], axis=1),
                qd=(q.astype(F32) * eg_col).astype(BF16),
                kd=(kf * ga['e_kd'][:, gj:gj + 1]).astype(BF16),
                e_tot=ga['e_tot'][:, gj:gj + 1]))

        for u in st:
            nm_bf = u['nm'].astype(BF16)
            u['r'] = u['nm']
            u['m'] = _dot(nm_bf, nm_bf)
        for _ in range(n_double - 1):
            for u in st:
                m_bf = u['m'].astype(BF16)
                rm = _dot(jnp.concatenate([u['r'].astype(BF16), m_bf], axis=0), m_bf)
                u['r'] = u['r'] + u['m'] + rm[:c]
                u['m'] = rm[c:]
        for u in st:
            u['r'] = u['r'] + u['m'] + _dot(u['r'].astype(BF16), u['m'].astype(BF16))
        for u in st:
            u['uw'] = u['xs'] + _dot(u['r'].astype(BF16), u['xs'].astype(BF16))
        for u in st:
            u['s'] = s_ref[u['j']]
            w_bf = u['uw'][:, DN_DV:].astype(BF16)
            u['wq'] = _dot(jnp.concatenate([w_bf, u['qd']], axis=0), u['s'].astype(BF16))
        for u in st:
            u['vn'] = (u['uw'][:, :DN_DV] - u['wq'][:c]).astype(BF16)
            s_ref[u['j']] = u['s'] * u['e_tot'] + _dot_tn(u['kd'], u['vn'])
        for u in st:
            o = u['wq'][c:] + _dot(u['qk'], u['vn'])
            dirs[u['d']][2][u['rows'], u['h'] * DN_DV:(u['h'] + 1) * DN_DV] = o
        return carry

    lax.fori_loop(0, nchunk, chunk_body, 0)


def _dn_scan_call(qkvn, ab, pvec):
    b, t_len, cols = qkvn.shape
    tt = min(TIME_TILE, t_len)
    nt = t_len // tt
    c = min(DN_CHUNK, tt)
    dv = DN_HEADS * DN_DV
    fwd = lambda n: pl.BlockSpec((None, tt, n), lambda b, i: (b, i, 0))
    bwd = lambda n: pl.BlockSpec((None, tt, n), lambda b, i: (b, nt - 1 - i, 0))
    return pl.pallas_call(
        functools.partial(_dn_scan_kernel, tt=tt, c=c),
        grid=(b, nt),
        in_specs=[fwd(cols), bwd(cols), fwd(LANES), bwd(LANES),
                  pl.BlockSpec(pvec.shape, lambda b, i: (0, 0))],
        out_specs=[fwd(dv), bwd(dv)],
        out_shape=[jax.ShapeDtypeStruct((b, t_len, dv), F32)] * 2,
        scratch_shapes=[pltpu.VMEM((2 * DN_HEADS, DN_DK, DN_DV), F32)],
        compiler_params=pltpu.CompilerParams(
            dimension_semantics=("parallel", "arbitrary"), vmem_limit_bytes=VMEM_LIMIT),
        name="dn_scan",
    )(qkvn, qkvn, ab, ab, pvec)


def _conf_kernel(prev_ref, cur_ref, next_ref, dw_ref, b_ref, g_ref, bb_ref, out_ref, sh_ref,
                 *, tt, halo):
    ch = out_ref.shape[1]

    def glu(x_ref):
        x = x_ref[...].astype(F32)
        return x[:, :ch] * jax.nn.sigmoid(x[:, ch:])

    _fill_with_halo(sh_ref.at[0], glu(prev_ref), glu(cur_ref), glu(next_ref), halo, tt)
    left = (CV_KERNEL - 1) // 2
    span = CONV_ROWS + 2 * halo - F32_SUBLANES
    for rb in range(tt // CONV_ROWS):
        r0 = rb * CONV_ROWS
        win = sh_ref[0, r0:r0 + CONV_ROWS + 2 * halo, :]
        for r in range(1, F32_SUBLANES):
            sh_ref[r, r0:r0 + span, :] = win[r:r + span, :]
        acc = None
        for k in range(CV_KERNEL):
            off = halo - left + k
            res = off % F32_SUBLANES
            base = r0 + off - res
            term = sh_ref[res, base:base + CONV_ROWS, :] * dw_ref[k:k + 1, :]
            acc = term if acc is None else acc + term
        acc = acc + b_ref[...]
        xc = acc - jnp.mean(acc, axis=-1, keepdims=True)
        var = jnp.mean(xc * xc, axis=-1, keepdims=True)
        y = xc * lax.rsqrt(var + EPS) * g_ref[...] + bb_ref[...]
        out_ref[r0:r0 + CONV_ROWS, :] = _silu(y).astype(BF16)


def _conf_call(glu_in, dw, dw_b, ln_g, ln_b):
    b, t_len, cols = glu_in.shape
    ch = cols // 2
    tt = min(TIME_TILE, t_len)
    halo = BF16_SUBLANES
    small = [dw, dw_b, ln_g, ln_b]
    return pl.pallas_call(
        functools.partial(_conf_kernel, tt=tt, halo=halo),
        grid=(b, t_len // tt),
        in_specs=_halo_specs(tt, halo, t_len, cols)
        + [pl.BlockSpec(a.shape, lambda b, i: (0, 0)) for a in small],
        out_specs=pl.BlockSpec((None, tt, ch), lambda b, i: (b, i, 0)),
        out_shape=jax.ShapeDtypeStruct((b, t_len, ch), BF16),
        scratch_shapes=[pltpu.VMEM((F32_SUBLANES, tt + 2 * halo, ch), F32)],
        compiler_params=pltpu.CompilerParams(
            dimension_semantics=("parallel", "parallel"), vmem_limit_bytes=VMEM_LIMIT),
        name="conformer",
    )(glu_in, glu_in, glu_in, *small)


def _attn_kernel(q_ref, kp_ref, kc_ref, kn_ref, qg_ref, kg_ref, sink_ref, out_ref,
                 kd_ref, vd_ref, bias_ref, *, tq, w):
    i = pl.program_id(1)
    nblk = pl.num_programs(1) * (tq // w)
    d = AT_HEAD_DIM
    lane = lax.broadcasted_iota(jnp.int32, (1, 2 * d), 1)
    lo = lane < d

    @pl.when(i == 0)
    def _():
        qi = lax.broadcasted_iota(jnp.int32, (w, 3 * w), 0)
        kj = lax.broadcasted_iota(jnp.int32, (w, 3 * w), 1)
        dist = jnp.abs(qi - (kj - w))
        for variant in range(4):
            ok = dist <= w
            if variant & 1:
                ok = ok & (kj >= w)
            if variant & 2:
                ok = ok & (kj < 2 * w)
            for h in range(AT_HEADS):
                slope = 2.0 ** (-ALIBI_MAX * (h + 1) / AT_HEADS)
                bias_ref[variant * AT_HEADS + h] = jnp.where(ok, -slope * dist.astype(F32), -jnp.inf)

    def head_rms(x, g):
        sq = x * x
        s_lo = jnp.sum(jnp.where(lo, sq, 0.0), axis=-1, keepdims=True)
        s_hi = jnp.sum(jnp.where(lo, 0.0, sq), axis=-1, keepdims=True)
        ms = jnp.where(lo, s_lo, s_hi) * (1.0 / d)
        return x * lax.rsqrt(ms + EPS) * g

    def fill_kv(src_ref, r0, rows):
        x = src_ref[...]
        kn = head_rms(x[:, :2 * d].astype(F32), kg_ref[...])
        vf = x[:, 2 * d:].astype(F32)
        k_roll = pltpu.roll(kn, d, axis=1)
        v_roll = pltpu.roll(vf, d, axis=1)
        kd_ref[0, r0:r0 + rows, :] = jnp.where(lo, kn, k_roll).astype(BF16)
        kd_ref[1, r0:r0 + rows, :] = jnp.where(lo, k_roll, kn).astype(BF16)
        vd_ref[0, r0:r0 + rows, :] = jnp.where(lo, vf, v_roll).astype(BF16)
        vd_ref[1, r0:r0 + rows, :] = jnp.where(lo, v_roll, vf).astype(BF16)

    fill_kv(kp_ref, 0, w)
    fill_kv(kc_ref, w, tq)
    fill_kv(kn_ref, w + tq, w)

    group = AT_HEADS // AT_KV_HEADS
    nsub = tq // w
    units = [(jb, g) for jb in range(nsub) for g in range(AT_KV_HEADS)]

    def scores(jb, g):
        qn = head_rms(q_ref[jb * w:(jb + 1) * w, g * 2 * d:(g + 1) * 2 * d].astype(F32),
                      qg_ref[...]) * (d ** -0.5)
        lhs = jnp.concatenate([jnp.where(lo, qn, 0.0), jnp.where(lo, 0.0, qn)], axis=0)
        return _dot_nt(lhs.astype(BF16), kd_ref[g, jb * w:jb * w + 3 * w, :])

    s_next = scores(*units[0])
    for idx, (jb, g) in enumerate(units):
        s = s_next
        if idx + 1 < len(units):
            s_next = scores(*units[idx + 1])
        n = i * nsub + jb
        variant = None
        if jb == 0:
            variant = (n == 0).astype(jnp.int32)
        if jb == nsub - 1:
            at_end = 2 * (n == nblk - 1).astype(jnp.int32)
            variant = at_end if variant is None else variant + at_end
        ps = []
        inv = []
        for hh in range(group):
            h = g * group + hh
            sink = sink_ref[h]
            bias = bias_ref[h] if variant is None else bias_ref[variant * AT_HEADS + h]
            sh = s[hh * w:(hh + 1) * w] + bias
            m = jnp.maximum(jnp.max(sh, axis=-1, keepdims=True), sink)
            p = jnp.exp(sh - m)
            den = jnp.sum(p, axis=-1, keepdims=True) + jnp.exp(sink - m)
            ps.append(p.astype(BF16))
            inv.append(1.0 / den)
        pv = _dot(jnp.concatenate(ps, axis=0), vd_ref[g, jb * w:jb * w + 3 * w, :])
        o = jnp.where(lo, pv[:w] * inv[0], pv[w:] * inv[1])
        out_ref[jb * w:(jb + 1) * w, g * 2 * d:(g + 1) * 2 * d] = o.astype(BF16)


def _attn_call(q, kv, q_g2, k_g2, sink):
    b, t_len, qc = q.shape
    w = AT_WINDOW
    tq = min(TIME_TILE, t_len)
    kvc = kv.shape[2]
    two_d = 2 * AT_HEAD_DIM
    return pl.pallas_call(
        functools.partial(_attn_kernel, tq=tq, w=w),
        grid=(b, t_len // tq),
        in_specs=[pl.BlockSpec((None, tq, qc), lambda b, i: (b, i, 0))]
        + _halo_specs(tq, w, t_len, kvc)
        + [pl.BlockSpec(q_g2.shape, lambda b, i: (0, 0)),
           pl.BlockSpec(k_g2.shape, lambda b, i: (0, 0)),
           pl.BlockSpec(memory_space=pltpu.SMEM)],
        out_specs=pl.BlockSpec((None, tq, qc), lambda b, i: (b, i, 0)),
        out_shape=jax.ShapeDtypeStruct((b, t_len, qc), BF16),
        scratch_shapes=[pltpu.VMEM((AT_KV_HEADS, tq + 2 * w, two_d), BF16),
                        pltpu.VMEM((AT_KV_HEADS, tq + 2 * w, two_d), BF16),
                        pltpu.VMEM((4 * AT_HEADS, w, 3 * w), F32)],
        compiler_params=pltpu.CompilerParams(
            dimension_semantics=("parallel", "arbitrary"), vmem_limit_bytes=VMEM_LIMIT),
        name="band_attn",
    )(q, kv, kv, kv, q_g2, k_g2, sink)


def _layer_params(l, ffn1_norm, ffn1_w_gate, ffn1_w_up, ffn1_w_down, mix_norm, w_in,
                  dn_conv, dn_a_log, dn_dt_bias, dn_out_norm, cv_dw, cv_dw_bias, cv_ln_g, cv_ln_b,
                  at_q_norm, at_k_norm, at_sink, w_out, ffn2_norm, ffn2_w_gate, ffn2_w_up,
                  ffn2_w_down, final_norm):
    row = lambda a: a[l].reshape(1, -1).astype(F32)
    bf = lambda a: a.astype(BF16)
    hk = DN_HEADS * DN_DK
    hv = DN_HEADS * DN_DV
    cv_ch = cv_dw.shape[2]
    sizes = (2 * hk + hv, hv, 2 * DN_HEADS, 2 * DN_HEADS, 2 * cv_ch,
             AT_HEADS * AT_HEAD_DIM, 2 * AT_KV_HEADS * AT_HEAD_DIM)
    offs = [0]
    for s in sizes:
        offs.append(offs[-1] + s)
    wi = w_in[l]
    piece = lambda n: wi[:, offs[n]:offs[n + 1]]
    n_gate = 4 * DN_HEADS
    w_ab = jnp.pad(jnp.concatenate([piece(2), piece(3)], axis=1), ((0, 0), (0, LANES - n_gate)))
    pad_gate = lambda a: jnp.pad(a.reshape(1, -1).astype(F32),
                                 ((0, 0), (2 * DN_HEADS, LANES - n_gate)))
    pvec = jnp.concatenate([pad_gate(dn_a_log[l]), pad_gate(dn_dt_bias[l])], axis=0)
    two = lambda a: jnp.concatenate([row(a), row(a)], axis=1)
    return dict(
        g1=row(ffn1_norm), wg1=bf(ffn1_w_gate[l]), wu1=bf(ffn1_w_up[l]), wd1=bf(ffn1_w_down[l]),
        gm=row(mix_norm), w_qkv=bf(piece(0)), w_z=bf(piece(1)), w_ab=bf(w_ab), w_glu=bf(piece(4)),
        w_q=bf(piece(5)), w_kv=bf(piece(6)),
        dn_conv=dn_conv[l].astype(F32), pvec=pvec, og=row(dn_out_norm),
        cv_dw=cv_dw[l].astype(F32), cv_b=row(cv_dw_bias), cv_g=row(cv_ln_g), cv_bb=row(cv_ln_b),
        q_g2=two(at_q_norm), k_g2=two(at_k_norm), sink=at_sink[l].astype(F32),
        w_out=bf(w_out[l]), g2=row(ffn2_norm), wg2=bf(ffn2_w_gate[l]), wu2=bf(ffn2_w_up[l]),
        wd2=bf(ffn2_w_down[l]), gf=row(final_norm))


def _layer(x, b, t_len, p):
    x1, qkv, z, glu, q_at, kv_at, ab = _ffn_in_call(x, p)
    seq = lambda a: a.reshape(b, t_len, a.shape[-1])
    flat = lambda a: a.reshape(b * t_len, a.shape[-1])
    qkvn = _dn_prep_call(seq(qkv), p['dn_conv'])
    o_f, o_b = _dn_scan_call(qkvn, seq(ab), p['pvec'])
    o_cv = _conf_call(seq(glu), p['cv_dw'], p['cv_b'], p['cv_g'], p['cv_bb'])
    o_at = _attn_call(seq(q_at), seq(kv_at), p['q_g2'], p['k_g2'], p['sink'])
    return _out_ffn_call(x1, flat(o_f), flat(o_b), z, flat(o_cv), flat(o_at), p)


def kernel(x_prompt, x_sample, ffn1_norm, ffn1_w_gate, ffn1_w_up, ffn1_w_down, mix_norm, w_in, dn_conv, dn_a_log, dn_dt_bias, dn_out_norm, cv_dw, cv_dw_bias, cv_ln_g, cv_ln_b, at_q_norm, at_k_norm, at_sink, w_out, ffn2_norm, ffn2_w_gate, ffn2_w_up, ffn2_w_down, final_norm):
    params = (ffn1_norm, ffn1_w_gate, ffn1_w_up, ffn1_w_down, mix_norm, w_in,
              dn_conv, dn_a_log, dn_dt_bias, dn_out_norm, cv_dw, cv_dw_bias, cv_ln_g, cv_ln_b,
              at_q_norm, at_k_norm, at_sink, w_out, ffn2_norm, ffn2_w_gate, ffn2_w_up, ffn2_w_down,
              final_norm)
    depth = ffn1_norm.shape[0]
    layers = [_layer_params(l, *params) for l in range(depth)]

    def trunk(x):
        b, t_len, d = x.shape
        y = x.reshape(b * t_len, d)
        for p in layers:
            y = _layer(y, b, t_len, p)
        return y.reshape(b, t_len, d)

    return (trunk(x_prompt), trunk(x_sample))
```

```python
import functools
import math

import jax
import jax.numpy as jnp
from jax import lax
from jax.experimental import pallas as pl
from jax.experimental.pallas import tpu as pltpu

F32 = jnp.float32
BF16 = jnp.bfloat16
EPS = 1e-6

DN_HEADS = 4
DN_DK = 128
DN_DV = 128
DN_CONV = 5
CV_KERNEL = 31
AT_HEADS = 4
AT_KV_HEADS = 2
AT_HEAD_DIM = 64
AT_WINDOW = 128
ALIBI_MAX = 8.0

LANES = 128
BF16_SUBLANES = 16
F32_SUBLANES = 8

ROW_TILE = 512
FF_CHUNK = 256
TIME_TILE = 512
DN_CHUNK = 128
CONV_ROWS = 64
HALO = BF16_SUBLANES
VMEM_LIMIT = 60 * 1024 * 1024


def _dot(a, b):
    return jnp.dot(a, b, preferred_element_type=F32)


def _dot_nt(a, b):
    return lax.dot_general(a, b, (((1,), (1,)), ((), ())), preferred_element_type=F32)


def _dot_tn(a, b):
    return lax.dot_general(a, b, (((0,), (0,)), ((), ())), preferred_element_type=F32)


def _rms(x, g):
    return x * lax.rsqrt(jnp.mean(x * x, axis=-1, keepdims=True) + EPS) * g


def _silu(x):
    return x * jax.nn.sigmoid(x)


def _softplus(x):
    return jnp.maximum(x, 0.0) + jnp.log1p(jnp.exp(-jnp.abs(x)))


def _const_spec(shape):
    nd = len(shape)
    return pl.BlockSpec(shape, lambda *_: (0,) * nd, pipeline_mode=pl.Buffered(1))


def _swiglu(h_bf, wg_ref, wu_ref, wd_ref, act_ref, side_tasks=()):
    d_ff, d = wd_ref.shape
    if not side_tasks:
        for c in range(0, d_ff, FF_CHUNK):
            gate = _dot(h_bf, wg_ref[:, c:c + FF_CHUNK])
            up = _dot(h_bf, wu_ref[:, c:c + FF_CHUNK])
            act_ref[:, c:c + FF_CHUNK] = (_silu(gate) * up).astype(BF16)
        return _dot(act_ref[...], wd_ref[...])

    tasks = list(side_tasks)
    stages = [('up', c) for c in range(0, d_ff, FF_CHUNK)] + [('down', c) for c in range(0, d, FF_CHUNK)]
    outs = []
    for idx, (kind, c) in enumerate(stages):
        if kind == 'up':
            gate = _dot(h_bf, wg_ref[:, c:c + FF_CHUNK])
            up = _dot(h_bf, wu_ref[:, c:c + FF_CHUNK])
            act_ref[:, c:c + FF_CHUNK] = (_silu(gate) * up).astype(BF16)
        else:
            outs.append(_dot(act_ref[...], wd_ref[:, c:c + FF_CHUNK]))
        take = -(-len(tasks) // (len(stages) - idx))
        for task in tasks[:take]:
            task()
        tasks = tasks[take:]
    return jnp.concatenate(outs, axis=1)


def _dn_prep_block(src_ref, w_ref, out_ref, r0, j):
    left = (DN_CONV - 1) // 2
    qk_groups = 2 * DN_HEADS * DN_DK // LANES
    q_groups = DN_HEADS * DN_DK // LANES
    ls = slice(j * LANES, (j + 1) * LANES)
    win = src_ref[r0:r0 + CONV_ROWS + 2 * HALO, ls].astype(F32)
    acc = None
    for k in range(DN_CONV):
        off = HALO - left + k
        term = win[off:off + CONV_ROWS, :] * w_ref[k:k + 1, ls]
        acc = term if acc is None else acc + term
    y = _silu(acc)
    if j < qk_groups:
        y = y * lax.rsqrt(jnp.sum(y * y, axis=-1, keepdims=True) + EPS)
    if j < q_groups:
        y = y * (DN_DK ** -0.5)
    out_ref[r0:r0 + CONV_ROWS, ls] = y.astype(BF16)


def _conformer_block(src_ref, dw_ref, b_ref, g_ref, bb_ref, out_ref, sh_ref, r0):
    ch = out_ref.shape[1]
    left = (CV_KERNEL - 1) // 2
    span = CONV_ROWS + 2 * HALO - F32_SUBLANES
    x = src_ref[r0:r0 + CONV_ROWS + 2 * HALO, :].astype(F32)
    win = x[:, :ch] * jax.nn.sigmoid(x[:, ch:])
    sh_ref[0] = win
    for r in range(1, F32_SUBLANES):
        sh_ref[r, 0:span, :] = win[r:r + span, :]
    acc = None
    for k in range(CV_KERNEL):
        off = HALO - left + k
        res = off % F32_SUBLANES
        term = sh_ref[res, off - res:off - res + CONV_ROWS, :] * dw_ref[k:k + 1, :]
        acc = term if acc is None else acc + term
    acc = acc + b_ref[...]
    xc = acc - jnp.mean(acc, axis=-1, keepdims=True)
    var = jnp.mean(xc * xc, axis=-1, keepdims=True)
    y = xc * lax.rsqrt(var + EPS) * g_ref[...] + bb_ref[...]
    out_ref[r0:r0 + CONV_ROWS, :] = _silu(y).astype(BF16)


def _ffn_in_kernel(xp_ref, xc_ref, xn_ref, g1_ref, wg_ref, wu_ref, wd_ref, gm_ref,
                   wqkv_ref, wglu_ref, wz_ref, wq_ref, wkv_ref, wab_ref,
                   cw_ref, dw_ref, cb_ref, cg_ref, cbb_ref,
                   x1_ref, z_ref, q_ref, kv_ref, ab_ref, qkvn_ref, ocv_ref,
                   act_ref, sq_ref, sg_ref, sh_ref, *, tm, nt):
    s = pl.program_id(0)
    n_tiles = pl.num_programs(0) - 1

    @pl.when(s == 0)
    def _():
        sq_ref[...] = jnp.zeros_like(sq_ref)
        sg_ref[...] = jnp.zeros_like(sg_ref)

    side = []
    for r0 in range(0, tm, CONV_ROWS):
        for j in range(sq_ref.shape[1] // LANES):
            side.append(functools.partial(_dn_prep_block, sq_ref, cw_ref, qkvn_ref, r0, j))
        side.append(functools.partial(_conformer_block, sg_ref, dw_ref, cb_ref, cg_ref, cbb_ref,
                                      ocv_ref, sh_ref, r0))

    x = jnp.concatenate([xp_ref[...], xc_ref[...], xn_ref[...]], axis=0)
    h = _rms(x, g1_ref[...]).astype(BF16)
    x1 = x + 0.5 * _swiglu(h, wg_ref, wu_ref, wd_ref, act_ref, side)
    x1_ref[...] = x1[HALO:HALO + tm]
    h2 = _rms(x1, gm_ref[...]).astype(BF16)

    i = jnp.minimum(s, n_tiles - 1) % nt
    rid = lax.broadcasted_iota(jnp.int32, (tm + 2 * HALO, 1), 0)
    keep = ((rid >= HALO) | (i > 0)) & ((rid < HALO + tm) | (i < nt - 1))
    for c in range(0, sq_ref.shape[1], sg_ref.shape[1]):
        cs = slice(c, c + sg_ref.shape[1])
        sq_ref[:, cs] = jnp.where(keep, _dot(h2, wqkv_ref[:, cs]), 0.0).astype(BF16)
    sg_ref[...] = jnp.where(keep, _dot(h2, wglu_ref[...]), 0.0).astype(BF16)
    hc = h2[HALO:HALO + tm]
    z_ref[...] = _dot(hc, wz_ref[...]).astype(BF16)
    q_ref[...] = _dot(hc, wq_ref[...]).astype(BF16)
    kv_ref[...] = _dot(hc, wkv_ref[...]).astype(BF16)
    ab_ref[...] = _dot(hc, wab_ref[...])


def _ffn_in_call(x, p, t_len):
    rows, d = x.shape
    tm = ROW_TILE
    assert t_len % tm == 0 and tm % CONV_ROWS == 0 and tm % HALO == 0
    n_tiles = rows // tm
    nt = t_len // tm
    per = tm // HALO
    d_ff = p['wg1'].shape[1]
    weights = [p['g1'], p['wg1'], p['wu1'], p['wd1'], p['gm'],
               p['w_qkv'], p['w_glu'], p['w_z'], p['w_q'], p['w_kv'], p['w_ab'],
               p['dn_conv'], p['cv_dw'], p['cv_b'], p['cv_g'], p['cv_bb']]
    qkv_cols = p['w_qkv'].shape[1]
    glu_cols = p['w_glu'].shape[1]
    cur = lambda s: jnp.minimum(s, n_tiles - 1)
    prv = lambda s: jnp.maximum(s - 1, 0)
    in_specs = [
        pl.BlockSpec((HALO, d), lambda s: (jnp.maximum(cur(s) * per - 1, 0), 0)),
        pl.BlockSpec((tm, d), lambda s: (cur(s), 0)),
        pl.BlockSpec((HALO, d), lambda s: (jnp.minimum((cur(s) + 1) * per, rows // HALO - 1), 0)),
    ] + [_const_spec(w.shape) for w in weights]
    outs = [(d, F32, cur), (p['w_z'].shape[1], BF16, cur), (p['w_q'].shape[1], BF16, cur),
            (p['w_kv'].shape[1], BF16, cur), (LANES, F32, cur),
            (qkv_cols, BF16, prv), (glu_cols // 2, BF16, prv)]
    out_spec = lambda n, tile: pl.BlockSpec((tm, n), lambda s: (tile(s), 0))
    return pl.pallas_call(
        functools.partial(_ffn_in_kernel, tm=tm, nt=nt),
        grid=(n_tiles + 1,),
        in_specs=in_specs,
        out_specs=[out_spec(n, tile) for n, _, tile in outs],
        out_shape=[jax.ShapeDtypeStruct((rows, n), dt) for n, dt, _ in outs],
        scratch_shapes=[pltpu.VMEM((tm + 2 * HALO, d_ff), BF16),
                        pltpu.VMEM((tm + 2 * HALO, qkv_cols), BF16),
                        pltpu.VMEM((tm + 2 * HALO, glu_cols), BF16),
                        pltpu.VMEM((F32_SUBLANES, CONV_ROWS + 2 * HALO, glu_cols // 2), F32)],
        compiler_params=pltpu.CompilerParams(
            dimension_semantics=("arbitrary",), vmem_limit_bytes=VMEM_LIMIT),
        name="ffn_in",
    )(x, x, x, *weights)


def _out_ffn_kernel(x1_ref, of_ref, ob_ref, z_ref, ocv_ref, oat_ref, og_ref, wo_ref,
                    g2_ref, wg_ref, wu_ref, wd_ref, gf_ref, y_ref, mix_ref, act_ref):
    dn_cols = of_ref.shape[1]
    for h in range(dn_cols // DN_DV):
        sl = slice(h * DN_DV, (h + 1) * DN_DV)
        o = _rms(of_ref[:, sl] + ob_ref[:, sl], og_ref[...])
        mix_ref[:, sl] = (o * _silu(z_ref[:, sl].astype(F32))).astype(BF16)
    cv_cols = ocv_ref.shape[1]
    mix_ref[:, dn_cols:dn_cols + cv_cols] = ocv_ref[...]
    mix_ref[:, dn_cols + cv_cols:] = oat_ref[...]
    x2 = x1_ref[...] + _dot(mix_ref[...], wo_ref[...])
    h2 = _rms(x2, g2_ref[...]).astype(BF16)
    x3 = x2 + 0.5 * _swiglu(h2, wg_ref, wu_ref, wd_ref, act_ref)
    y_ref[...] = _rms(x3, gf_ref[...])


def _out_ffn_call(x1, o_f, o_b, z, o_cv, o_at, p):
    rows, d = x1.shape
    tm = min(ROW_TILE, rows)
    d_ff = p['wg2'].shape[1]
    weights = [p['og'], p['w_out'], p['g2'], p['wg2'], p['wu2'], p['wd2'], p['gf']]
    acts = [x1, o_f, o_b, z, o_cv, o_at]
    row_spec = lambda n: pl.BlockSpec((tm, n), lambda i: (i, 0))
    return pl.pallas_call(
        _out_ffn_kernel,
        grid=(rows // tm,),
        in_specs=[row_spec(a.shape[1]) for a in acts] + [_const_spec(w.shape) for w in weights],
        out_specs=row_spec(d),
        out_shape=jax.ShapeDtypeStruct((rows, d), F32),
        scratch_shapes=[pltpu.VMEM((tm, d), BF16), pltpu.VMEM((tm, d_ff), BF16)],
        compiler_params=pltpu.CompilerParams(
            dimension_semantics=("parallel",), vmem_limit_bytes=VMEM_LIMIT),
        name="out_ffn",
    )(*acts, *weights)


def _split_dot(m_bf, x):
    hi = x.astype(BF16)
    r1 = x - hi.astype(F32)
    mid = r1.astype(BF16)
    lo = (r1 - mid.astype(F32)).astype(BF16)
    return _dot(m_bf, hi) + _dot(m_bf, mid) + _dot(m_bf, lo)


def _dn_scan_kernel(qf_ref, qb_ref, abf_ref, abb_ref, pv_ref, of_ref, ob_ref, s_ref, *, tt, c):
    @pl.when(pl.program_id(1) == 0)
    def _():
        s_ref[...] = jnp.zeros_like(s_ref)

    nchunk = tt // c
    row = lax.broadcasted_iota(jnp.int32, (c, c), 0)
    col = lax.broadcasted_iota(jnp.int32, (c, c), 1)
    incl = (row >= col, row <= col)
    strict = (row > col, row < col)
    tri_bf = (incl[0].astype(BF16), incl[1].astype(BF16))
    dirs = ((qf_ref, abf_ref, of_ref), (qb_ref, abb_ref, ob_ref))
    hk = DN_HEADS * DN_DK
    gate_lane = 2 * DN_HEADS
    n_double = int(math.log2(c)) - 1

    units = [(d, h) for d in range(2) for h in range(DN_HEADS)]

    def chunk_body(n, carry):
        gates = []
        for d in range(2):
            nn = n if d == 0 else nchunk - 1 - n
            r0 = pl.multiple_of(nn * c, c)
            ab = dirs[d][1][pl.ds(r0, c), :]
            beta = jax.nn.sigmoid(ab)
            g = -jnp.exp(pv_ref[0:1, :]) * _softplus(ab + pv_ref[1:2, :])
            gc = _split_dot(tri_bf[d], g)
            tot = gc[c - 1:c, :] if d == 0 else gc[0:1, :]
            gates.append(dict(r0=r0, beta=beta, gc=gc, gc_t=gc.T, e_gc=jnp.exp(gc),
                              e_kd=jnp.exp(tot - gc), e_tot=jnp.exp(tot)))

        st = []
        for d, h in units:
            ga = gates[d]
            x_ref = dirs[d][0]
            j = d * DN_HEADS + h
            gj = gate_lane + j
            rows = pl.ds(ga['r0'], c)
            q = x_ref[rows, h * DN_DK:(h + 1) * DN_DK]
            k = x_ref[rows, hk + h * DN_DK:hk + (h + 1) * DN_DK]
            v = x_ref[rows, 2 * hk + h * DN_DV:2 * hk + (h + 1) * DN_DV]
            kf = k.astype(F32)
            b_col = ga['beta'][:, j:j + 1]
            kb = kf * b_col
            eg_col = ga['e_gc'][:, gj:gj + 1]
            diff = ga['gc'][:, gj:gj + 1] - ga['gc_t'][gj:gj + 1, :]
            decay = jnp.exp(jnp.where(incl[d], diff, -jnp.inf))
            gq = _dot_nt(jnp.concatenate([kb.astype(BF16), q], axis=0), k)
            st.append(dict(
                d=d, h=h, j=j, rows=rows,
                nm=jnp.where(strict[d], -(gq[:c] * decay), 0.0),
                qk=(gq[c:] * decay).astype(BF16),
                xs=jnp.concatenate([v.astype(F32) * b_col, kb * eg_col], axis=1),
                qd=(q.astype(F32) * eg_col).astype(BF16),
                kd=(kf * ga['e_kd'][:, gj:gj + 1]).astype(BF16),
                e_tot=ga['e_tot'][:, gj:gj + 1]))

        for u in st:
            nm_bf = u['nm'].astype(BF16)
            u['r'] = u['nm']
            u['m'] = _dot(nm_bf, nm_bf)
        for _ in range(n_double - 1):
            for u in st:
                m_bf = u['m'].astype(BF16)
                rm = _dot(jnp.concatenate([u['r'].astype(BF16), m_bf], axis=0), m_bf)
                u['r'] = u['r'] + u['m'] + rm[:c]
                u['m'] = rm[c:]
        for u in st:
            u['r'] = u['r'] + u['m'] + _dot(u['r'].astype(BF16), u['m'].astype(BF16))
        for u in st:
            u['uw'] = u['xs'] + _dot(u['r'].astype(BF16), u['xs'].astype(BF16))
        for u in st:
            u['s'] = s_ref[u['j']]
            w_bf = u['uw'][:, DN_DV:].astype(BF16)
            u['wq'] = _dot(jnp.concatenate([w_bf, u['qd']], axis=0), u['s'].astype(BF16))
        for u in st:
            u['vn'] = (u['uw'][:, :DN_DV] - u['wq'][:c]).astype(BF16)
            s_ref[u['j']] = u['s'] * u['e_tot'] + _dot_tn(u['kd'], u['vn'])
        for u in st:
            o = u['wq'][c:] + _dot(u['qk'], u['vn'])
            dirs[u['d']][2][u['rows'], u['h'] * DN_DV:(u['h'] + 1) * DN_DV] = o
        return carry

    lax.fori_loop(0, nchunk, chunk_body, 0, unroll=True)


def _dn_scan_call(qkvn, ab, pvec):
    b, t_len, cols = qkvn.shape
    tt = min(TIME_TILE, t_len)
    nt = t_len // tt
    c = min(DN_CHUNK, tt)
    dv = DN_HEADS * DN_DV
    fwd = lambda n: pl.BlockSpec((None, tt, n), lambda b, i: (b, i, 0))
    bwd = lambda n: pl.BlockSpec((None, tt, n), lambda b, i: (b, nt - 1 - i, 0))
    return pl.pallas_call(
        functools.partial(_dn_scan_kernel, tt=tt, c=c),
        grid=(b, nt),
        in_specs=[fwd(cols), bwd(cols), fwd(LANES), bwd(LANES),
                  pl.BlockSpec(pvec.shape, lambda b, i: (0, 0))],
        out_specs=[fwd(dv), bwd(dv)],
        out_shape=[jax.ShapeDtypeStruct((b, t_len, dv), F32)] * 2,
        scratch_shapes=[pltpu.VMEM((2 * DN_HEADS, DN_DK, DN_DV), F32)],
        compiler_params=pltpu.CompilerParams(
            dimension_semantics=("parallel", "arbitrary"), vmem_limit_bytes=VMEM_LIMIT),
        name="dn_scan",
    )(qkvn, qkvn, ab, ab, pvec)


def _halo_specs(tt, halo, t_len, cols):
    per = tt // halo
    last = t_len // halo - 1
    prev = pl.BlockSpec((None, halo, cols), lambda b, i: (b, jnp.maximum(i * per - 1, 0), 0))
    cur = pl.BlockSpec((None, tt, cols), lambda b, i: (b, i, 0))
    nxt = pl.BlockSpec((None, halo, cols), lambda b, i: (b, jnp.minimum((i + 1) * per, last), 0))
    return [prev, cur, nxt]


def _attn_kernel(q_ref, kp_ref, kc_ref, kn_ref, qg_ref, kg_ref, sink_ref, out_ref,
                 kd_ref, vd_ref, bias_ref, *, tq, w):
    i = pl.program_id(1)
    nblk = pl.num_programs(1) * (tq // w)
    d = AT_HEAD_DIM
    lane = lax.broadcasted_iota(jnp.int32, (1, 2 * d), 1)
    lo = lane < d

    @pl.when(i == 0)
    def _():
        qi = lax.broadcasted_iota(jnp.int32, (w, 3 * w), 0)
        kj = lax.broadcasted_iota(jnp.int32, (w, 3 * w), 1)
        dist = jnp.abs(qi - (kj - w))
        for variant in range(4):
            ok = dist <= w
            if variant & 1:
                ok = ok & (kj >= w)
            if variant & 2:
                ok = ok & (kj < 2 * w)
            for h in range(AT_HEADS):
                slope = 2.0 ** (-ALIBI_MAX * (h + 1) / AT_HEADS)
                bias_ref[variant * AT_HEADS + h] = jnp.where(ok, -slope * dist.astype(F32), -jnp.inf)

    def head_rms(x, g):
        sq = x * x
        s_lo = jnp.sum(jnp.where(lo, sq, 0.0), axis=-1, keepdims=True)
        s_hi = jnp.sum(jnp.where(lo, 0.0, sq), axis=-1, keepdims=True)
        ms = jnp.where(lo, s_lo, s_hi) * (1.0 / d)
        return x * lax.rsqrt(ms + EPS) * g

    def fill_kv(src_ref, r0, rows):
        x = src_ref[...]
        kn = head_rms(x[:, :2 * d].astype(F32), kg_ref[...])
        vf = x[:, 2 * d:].astype(F32)
        k_roll = pltpu.roll(kn, d, axis=1)
        v_roll = pltpu.roll(vf, d, axis=1)
        kd_ref[0, r0:r0 + rows, :] = jnp.where(lo, kn, k_roll).astype(BF16)
        kd_ref[1, r0:r0 + rows, :] = jnp.where(lo, k_roll, kn).astype(BF16)
        vd_ref[0, r0:r0 + rows, :] = jnp.where(lo, vf, v_roll).astype(BF16)
        vd_ref[1, r0:r0 + rows, :] = jnp.where(lo, v_roll, vf).astype(BF16)

    fill_kv(kp_ref, 0, w)
    fill_kv(kc_ref, w, tq)
    fill_kv(kn_ref, w + tq, w)

    group = AT_HEADS // AT_KV_HEADS
    nsub = tq // w
    units = [(jb, g) for jb in range(nsub) for g in range(AT_KV_HEADS)]

    def scores(jb, g):
        qn = head_rms(q_ref[jb * w:(jb + 1) * w, g * 2 * d:(g + 1) * 2 * d].astype(F32),
                      qg_ref[...]) * (d ** -0.5)
        lhs = jnp.concatenate([jnp.where(lo, qn, 0.0), jnp.where(lo, 0.0, qn)], axis=0)
        return _dot_nt(lhs.astype(BF16), kd_ref[g, jb * w:jb * w + 3 * w, :])

    s_next = scores(*units[0])
    for idx, (jb, g) in enumerate(units):
        s = s_next
        if idx + 1 < len(units):
            s_next = scores(*units[idx + 1])
        n = i * nsub + jb
        variant = None
        if jb == 0:
            variant = (n == 0).astype(jnp.int32)
        if jb == nsub - 1:
            at_end = 2 * (n == nblk - 1).astype(jnp.int32)
            variant = at_end if variant is None else variant + at_end
        ps = []
        inv = []
        for hh in range(group):
            h = g * group + hh
            sink = sink_ref[h]
            bias = bias_ref[h] if variant is None else bias_ref[variant * AT_HEADS + h]
            sh = s[hh * w:(hh + 1) * w] + bias
            m = jnp.maximum(jnp.max(sh, axis=-1, keepdims=True), sink)
            p = jnp.exp(sh - m)
            den = jnp.sum(p, axis=-1, keepdims=True) + jnp.exp(sink - m)
            ps.append(p.astype(BF16))
            inv.append(1.0 / den)
        pv = _dot(jnp.concatenate(ps, axis=0), vd_ref[g, jb * w:jb * w + 3 * w, :])
        o = jnp.where(lo, pv[:w] * inv[0], pv[w:] * inv[1])
        out_ref[jb * w:(jb + 1) * w, g * 2 * d:(g + 1) * 2 * d] = o.astype(BF16)


def _attn_call(q, kv, q_g2, k_g2, sink):
    b, t_len, qc = q.shape
    w = AT_WINDOW
    tq = min(TIME_TILE, t_len)
    kvc = kv.shape[2]
    two_d = 2 * AT_HEAD_DIM
    return pl.pallas_call(
        functools.partial(_attn_kernel, tq=tq, w=w),
        grid=(b, t_len // tq),
        in_specs=[pl.BlockSpec((None, tq, qc), lambda b, i: (b, i, 0))]
        + _halo_specs(tq, w, t_len, kvc)
        + [pl.BlockSpec(q_g2.shape, lambda b, i: (0, 0)),
           pl.BlockSpec(k_g2.shape, lambda b, i: (0, 0)),
           pl.BlockSpec(memory_space=pltpu.SMEM)],
        out_specs=pl.BlockSpec((None, tq, qc), lambda b, i: (b, i, 0)),
        out_shape=jax.ShapeDtypeStruct((b, t_len, qc), BF16),
        scratch_shapes=[pltpu.VMEM((AT_KV_HEADS, tq + 2 * w, two_d), BF16),
                        pltpu.VMEM((AT_KV_HEADS, tq + 2 * w, two_d), BF16),
                        pltpu.VMEM((4 * AT_HEADS, w, 3 * w), F32)],
        compiler_params=pltpu.CompilerParams(
            dimension_semantics=("parallel", "arbitrary"), vmem_limit_bytes=VMEM_LIMIT),
        name="band_attn",
    )(q, kv, kv, kv, q_g2, k_g2, sink)


def _layer_params(l, ffn1_norm, ffn1_w_gate, ffn1_w_up, ffn1_w_down, mix_norm, w_in,
                  dn_conv, dn_a_log, dn_dt_bias, dn_out_norm, cv_dw, cv_dw_bias, cv_ln_g, cv_ln_b,
                  at_q_norm, at_k_norm, at_sink, w_out, ffn2_norm, ffn2_w_gate, ffn2_w_up,
                  ffn2_w_down, final_norm):
    row = lambda a: a[l].reshape(1, -1).astype(F32)
    bf = lambda a: a.astype(BF16)
    hk = DN_HEADS * DN_DK
    hv = DN_HEADS * DN_DV
    cv_ch = cv_dw.shape[2]
    sizes = (2 * hk + hv, hv, 2 * DN_HEADS, 2 * DN_HEADS, 2 * cv_ch,
             AT_HEADS * AT_HEAD_DIM, 2 * AT_KV_HEADS * AT_HEAD_DIM)
    offs = [0]
    for s in sizes:
        offs.append(offs[-1] + s)
    wi = w_in[l]
    piece = lambda n: wi[:, offs[n]:offs[n + 1]]
    n_gate = 4 * DN_HEADS
    w_ab = jnp.pad(jnp.concatenate([piece(2), piece(3)], axis=1), ((0, 0), (0, LANES - n_gate)))
    pad_gate = lambda a: jnp.pad(a.reshape(1, -1).astype(F32),
                                 ((0, 0), (2 * DN_HEADS, LANES - n_gate)))
    pvec = jnp.concatenate([pad_gate(dn_a_log[l]), pad_gate(dn_dt_bias[l])], axis=0)
    two = lambda a: jnp.concatenate([row(a), row(a)], axis=1)
    return dict(
        g1=row(ffn1_norm), wg1=bf(ffn1_w_gate[l]), wu1=bf(ffn1_w_up[l]), wd1=bf(ffn1_w_down[l]),
        gm=row(mix_norm), w_qkv=bf(piece(0)), w_z=bf(piece(1)), w_ab=bf(w_ab), w_glu=bf(piece(4)),
        w_q=bf(piece(5)), w_kv=bf(piece(6)),
        dn_conv=dn_conv[l].astype(F32), pvec=pvec, og=row(dn_out_norm),
        cv_dw=cv_dw[l].astype(F32), cv_b=row(cv_dw_bias), cv_g=row(cv_ln_g), cv_bb=row(cv_ln_b),
        q_g2=two(at_q_norm), k_g2=two(at_k_norm), sink=at_sink[l].astype(F32),
        w_out=bf(w_out[l]), g2=row(ffn2_norm), wg2=bf(ffn2_w_gate[l]), wu2=bf(ffn2_w_up[l]),
        wd2=bf(ffn2_w_down[l]), gf=row(final_norm))


def _layer(x, b, t_len, p):
    x1, z, q_at, kv_at, ab, qkvn, o_cv = _ffn_in_call(x, p, t_len)
    seq = lambda a: a.reshape(b, t_len, a.shape[-1])
    flat = lambda a: a.reshape(b * t_len, a.shape[-1])
    o_f, o_b = _dn_scan_call(seq(qkvn), seq(ab), p['pvec'])
    o_at = _attn_call(seq(q_at), seq(kv_at), p['q_g2'], p['k_g2'], p['sink'])
    return _out_ffn_call(x1, flat(o_f), flat(o_b), z, o_cv, flat(o_at), p)


def kernel(x_prompt, x_sample, ffn1_norm, ffn1_w_gate, ffn1_w_up, ffn1_w_down, mix_norm, w_in, dn_conv, dn_a_log, dn_dt_bias, dn_out_norm, cv_dw, cv_dw_bias, cv_ln_g, cv_ln_b, at_q_norm, at_k_norm, at_sink, w_out, ffn2_norm, ffn2_w_gate, ffn2_w_up, ffn2_w_down, final_norm):
    params = (ffn1_norm, ffn1_w_gate, ffn1_w_up, ffn1_w_down, mix_norm, w_in,
              dn_conv, dn_a_log, dn_dt_bias, dn_out_norm, cv_dw, cv_dw_bias, cv_ln_g, cv_ln_b,
              at_q_norm, at_k_norm, at_sink, w_out, ffn2_norm, ffn2_w_gate, ffn2_w_up, ffn2_w_down,
              final_norm)
    depth = ffn1_norm.shape[0]
    layers = [_layer_params(l, *params) for l in range(depth)]

    def trunk(x):
        b, t_len, d = x.shape
        y = x.reshape(b * t_len, d)
        for p in layers:
            y = _layer(y, b, t_len, p)
        return y.reshape(b, t_len, d)

    return (trunk(x_prompt), trunk(x_sample))
```

```python
import functools
import math

import jax
import jax.numpy as jnp
from jax import lax
from jax.experimental import pallas as pl
from jax.experimental.pallas import tpu as pltpu

F32 = jnp.float32
BF16 = jnp.bfloat16
EPS = 1e-6

DN_HEADS = 4
DN_DK = 128
DN_DV = 128
DN_CONV = 5
CV_KERNEL = 31
AT_HEADS = 4
AT_KV_HEADS = 2
AT_HEAD_DIM = 64
AT_WINDOW = 128
ALIBI_MAX = 8.0
LOG2E = 1.4426950408889634

LANES = 128
BF16_SUBLANES = 16
F32_SUBLANES = 8

ROW_TILE = 512
FF_CHUNK = 256
TIME_TILE = 512
DN_CHUNK = 128
CONV_ROWS = 64
VMEM_LIMIT = 56 * 1024 * 1024


def _dot(a, b):
    return jnp.dot(a, b, preferred_element_type=F32)


def _dot_nt(a, b):
    return lax.dot_general(a, b, (((1,), (1,)), ((), ())), preferred_element_type=F32)


def _dot_tn(a, b):
    return lax.dot_general(a, b, (((0,), (0,)), ((), ())), preferred_element_type=F32)


def _rms(x, g):
    return x * lax.rsqrt(jnp.mean(x * x, axis=-1, keepdims=True) + EPS) * g


def _silu(x):
    return x * jax.nn.sigmoid(x)


def _softplus(x):
    return jnp.maximum(x, 0.0) + jnp.log1p(jnp.exp(-jnp.abs(x)))


def _const_spec(shape):
    nd = len(shape)
    return pl.BlockSpec(shape, lambda *_: (0,) * nd, pipeline_mode=pl.Buffered(1))


def _swiglu(h_bf, wg_ref, wu_ref, wd_ref, act_ref):
    d_ff = wg_ref.shape[1]
    for c in range(0, d_ff, FF_CHUNK):
        gate = _dot(h_bf, wg_ref[:, c:c + FF_CHUNK])
        up = _dot(h_bf, wu_ref[:, c:c + FF_CHUNK])
        act_ref[:, c:c + FF_CHUNK] = (_silu(gate) * up).astype(BF16)
    return _dot(act_ref[...], wd_ref[...])


def _ffn_in_kernel(x_ref, g1_ref, wg_ref, wu_ref, wd_ref, gm_ref,
                   wqkv_ref, wz_ref, wglu_ref, wq_ref, wkv_ref, wab_ref,
                   x1_ref, qkv_ref, z_ref, glu_ref, q_ref, kv_ref, ab_ref,
                   act_ref):
    x = x_ref[...]
    h = _rms(x, g1_ref[...]).astype(BF16)
    x1 = x + 0.5 * _swiglu(h, wg_ref, wu_ref, wd_ref, act_ref)
    x1_ref[...] = x1
    h2 = _rms(x1, gm_ref[...]).astype(BF16)
    qkv_ref[...] = _dot(h2, wqkv_ref[...]).astype(BF16)
    z_ref[...] = _dot(h2, wz_ref[...]).astype(BF16)
    glu_ref[...] = _dot(h2, wglu_ref[...]).astype(BF16)
    q_ref[...] = _dot(h2, wq_ref[...]).astype(BF16)
    kv_ref[...] = _dot(h2, wkv_ref[...]).astype(BF16)
    ab_ref[...] = _dot(h2, wab_ref[...])


def _ffn_in_call(x, p):
    rows, d = x.shape
    tm = min(ROW_TILE, rows)
    d_ff = p['wg1'].shape[1]
    weights = [p['g1'], p['wg1'], p['wu1'], p['wd1'], p['gm'],
               p['w_qkv'], p['w_z'], p['w_glu'], p['w_q'], p['w_kv'], p['w_ab']]
    out_cols = [(d, F32)] + [(w.shape[1], BF16) for w in weights[5:10]] + [(LANES, F32)]
    row_spec = lambda n: pl.BlockSpec((tm, n), lambda i: (i, 0))
    return pl.pallas_call(
        _ffn_in_kernel,
        grid=(rows // tm,),
        in_specs=[row_spec(d)] + [_const_spec(w.shape) for w in weights],
        out_specs=[row_spec(n) for n, _ in out_cols],
        out_shape=[jax.ShapeDtypeStruct((rows, n), dt) for n, dt in out_cols],
        scratch_shapes=[pltpu.VMEM((tm, d_ff), BF16)],
        compiler_params=pltpu.CompilerParams(
            dimension_semantics=("parallel",), vmem_limit_bytes=VMEM_LIMIT),
        name="ffn_in",
    )(x, *weights)


def _out_ffn_kernel(x1_ref, of_ref, ob_ref, z_ref, ocv_ref, oat_ref, og_ref, wo_ref,
                    g2_ref, wg_ref, wu_ref, wd_ref, gf_ref, y_ref, mix_ref, act_ref):
    dn_cols = of_ref.shape[1]
    for h in range(dn_cols // DN_DV):
        sl = slice(h * DN_DV, (h + 1) * DN_DV)
        o = _rms(of_ref[:, sl] + ob_ref[:, sl], og_ref[...])
        mix_ref[:, sl] = (o * _silu(z_ref[:, sl].astype(F32))).astype(BF16)
    cv_cols = ocv_ref.shape[1]
    mix_ref[:, dn_cols:dn_cols + cv_cols] = ocv_ref[...]
    mix_ref[:, dn_cols + cv_cols:] = oat_ref[...]
    x2 = x1_ref[...] + _dot(mix_ref[...], wo_ref[...])
    h2 = _rms(x2, g2_ref[...]).astype(BF16)
    x3 = x2 + 0.5 * _swiglu(h2, wg_ref, wu_ref, wd_ref, act_ref)
    y_ref[...] = _rms(x3, gf_ref[...])


def _out_ffn_call(x1, o_f, o_b, z, o_cv, o_at, p):
    rows, d = x1.shape
    tm = min(ROW_TILE, rows)
    d_ff = p['wg2'].shape[1]
    weights = [p['og'], p['w_out'], p['g2'], p['wg2'], p['wu2'], p['wd2'], p['gf']]
    acts = [x1, o_f, o_b, z, o_cv, o_at]
    row_spec = lambda n: pl.BlockSpec((tm, n), lambda i: (i, 0))
    return pl.pallas_call(
        _out_ffn_kernel,
        grid=(rows // tm,),
        in_specs=[row_spec(a.shape[1]) for a in acts] + [_const_spec(w.shape) for w in weights],
        out_specs=row_spec(d),
        out_shape=jax.ShapeDtypeStruct((rows, d), F32),
        scratch_shapes=[pltpu.VMEM((tm, d), BF16), pltpu.VMEM((tm, d_ff), BF16)],
        compiler_params=pltpu.CompilerParams(
            dimension_semantics=("parallel",), vmem_limit_bytes=VMEM_LIMIT),
        name="out_ffn",
    )(*acts, *weights)


def _halo_specs(tt, halo, t_len, cols):
    per = tt // halo
    last = t_len // halo - 1
    prev = pl.BlockSpec((None, halo, cols), lambda b, i: (b, jnp.maximum(i * per - 1, 0), 0))
    cur = pl.BlockSpec((None, tt, cols), lambda b, i: (b, i, 0))
    nxt = pl.BlockSpec((None, halo, cols), lambda b, i: (b, jnp.minimum((i + 1) * per, last), 0))
    return [prev, cur, nxt]


def _fill_with_halo(dst_ref, prev, cur, nxt, halo, tt):
    i = pl.program_id(1)
    last = pl.num_programs(1) - 1
    dst_ref[0:halo, :] = jnp.where(i > 0, prev, jnp.zeros_like(prev))
    dst_ref[halo:halo + tt, :] = cur
    dst_ref[halo + tt:halo + tt + halo, :] = jnp.where(i < last, nxt, jnp.zeros_like(nxt))


def _dn_prep_kernel(prev_ref, cur_ref, next_ref, w_ref, out_ref, xw_ref, *, tt, halo):
    _fill_with_halo(xw_ref, prev_ref[...], cur_ref[...], next_ref[...], halo, tt)
    cols = cur_ref.shape[1]
    left = (DN_CONV - 1) // 2
    qk_groups = 2 * DN_HEADS * DN_DK // LANES
    q_groups = DN_HEADS * DN_DK // LANES
    win_rows = CONV_ROWS + 2 * halo
    out_row = lax.broadcasted_iota(jnp.int32, (DN_CONV * CONV_ROWS, win_rows), 0)
    src_row = lax.broadcasted_iota(jnp.int32, (DN_CONV * CONV_ROWS, win_rows), 1)
    tap = out_row // CONV_ROWS
    shift = (src_row == (out_row - tap * CONV_ROWS) + (halo - left) + tap).astype(BF16)
    group = 2 * LANES
    for rb in range(tt // CONV_ROWS):
        r0 = rb * CONV_ROWS
        for j in range(cols // group):
            gs = slice(j * group, (j + 1) * group)
            taps = _dot(shift, xw_ref[r0:r0 + win_rows, gs])
            acc = None
            for k in range(DN_CONV):
                term = taps[k * CONV_ROWS:(k + 1) * CONV_ROWS, :] * w_ref[k:k + 1, gs]
                acc = term if acc is None else acc + term
            y = _silu(acc)
            for e in range(group // LANES):
                jj = j * (group // LANES) + e
                ye = y[:, e * LANES:(e + 1) * LANES]
                if jj < qk_groups:
                    ye = ye * lax.rsqrt(jnp.sum(ye * ye, axis=-1, keepdims=True) + EPS)
                if jj < q_groups:
                    ye = ye * (DN_DK ** -0.5)
                out_ref[r0:r0 + CONV_ROWS, jj * LANES:(jj + 1) * LANES] = ye.astype(BF16)


def _dn_prep_call(qkv, conv_w):
    b, t_len, cols = qkv.shape
    tt = min(TIME_TILE, t_len)
    halo = BF16_SUBLANES
    return pl.pallas_call(
        functools.partial(_dn_prep_kernel, tt=tt, halo=halo),
        grid=(b, t_len // tt),
        in_specs=_halo_specs(tt, halo, t_len, cols) + [pl.BlockSpec(conv_w.shape, lambda b, i: (0, 0))],
        out_specs=pl.BlockSpec((None, tt, cols), lambda b, i: (b, i, 0)),
        out_shape=jax.ShapeDtypeStruct((b, t_len, cols), BF16),
        scratch_shapes=[pltpu.VMEM((tt + 2 * halo, cols), BF16)],
        compiler_params=pltpu.CompilerParams(
            dimension_semantics=("parallel", "parallel"), vmem_limit_bytes=VMEM_LIMIT),
        name="dn_prep",
    )(qkv, qkv, qkv, conv_w)


def _conf_kernel(prev_ref, cur_ref, next_ref, dw_ref, b_ref, g_ref, bb_ref, out_ref, sh_ref,
                 *, tt, halo):
    ch = out_ref.shape[1]

    def glu(x_ref):
        x = x_ref[...].astype(F32)
        return x[:, :ch] * jax.nn.sigmoid(x[:, ch:])

    _fill_with_halo(sh_ref.at[0], glu(prev_ref), glu(cur_ref), glu(next_ref), halo, tt)
    left = (CV_KERNEL - 1) // 2
    span = CONV_ROWS + 2 * halo - F32_SUBLANES
    for rb in range(tt // CONV_ROWS):
        r0 = rb * CONV_ROWS
        win = sh_ref[0, r0:r0 + CONV_ROWS + 2 * halo, :]
        for r in range(1, F32_SUBLANES):
            sh_ref[r, r0:r0 + span, :] = win[r:r + span, :]
        acc = None
        for k in range(CV_KERNEL):
            off = halo - left + k
            res = off % F32_SUBLANES
            base = r0 + off - res
            term = sh_ref[res, base:base + CONV_ROWS, :] * dw_ref[k:k + 1, :]
            acc = term if acc is None else acc + term
        acc = acc + b_ref[...]
        xc = acc - jnp.mean(acc, axis=-1, keepdims=True)
        var = jnp.mean(xc * xc, axis=-1, keepdims=True)
        y = xc * lax.rsqrt(var + EPS) * g_ref[...] + bb_ref[...]
        out_ref[r0:r0 + CONV_ROWS, :] = _silu(y).astype(BF16)


def _conf_call(glu_in, dw, dw_b, ln_g, ln_b):
    b, t_len, cols = glu_in.shape
    ch = cols // 2
    tt = min(TIME_TILE, t_len)
    halo = BF16_SUBLANES
    small = [dw, dw_b, ln_g, ln_b]
    return pl.pallas_call(
        functools.partial(_conf_kernel, tt=tt, halo=halo),
        grid=(b, t_len // tt),
        in_specs=_halo_specs(tt, halo, t_len, cols)
        + [pl.BlockSpec(a.shape, lambda b, i: (0, 0)) for a in small],
        out_specs=pl.BlockSpec((None, tt, ch), lambda b, i: (b, i, 0)),
        out_shape=jax.ShapeDtypeStruct((b, t_len, ch), BF16),
        scratch_shapes=[pltpu.VMEM((F32_SUBLANES, tt + 2 * halo, ch), F32)],
        compiler_params=pltpu.CompilerParams(
            dimension_semantics=("parallel", "parallel"), vmem_limit_bytes=VMEM_LIMIT),
        name="conformer",
    )(glu_in, glu_in, glu_in, *small)


def _split_dot(m_bf, x):
    hi = x.astype(BF16)
    r1 = x - hi.astype(F32)
    mid = r1.astype(BF16)
    lo = (r1 - mid.astype(F32)).astype(BF16)
    return _dot(m_bf, hi) + _dot(m_bf, mid) + _dot(m_bf, lo)


def _dn_scan_kernel(qf_ref, qb_ref, abf_ref, abb_ref, pv_ref, of_ref, ob_ref, s_ref, *, tt, c):
    @pl.when(pl.program_id(1) == 0)
    def _():
        s_ref[...] = jnp.zeros_like(s_ref)

    nchunk = tt // c
    row = lax.broadcasted_iota(jnp.int32, (c, c), 0)
    col = lax.broadcasted_iota(jnp.int32, (c, c), 1)
    incl = (row >= col, row <= col)
    strict = (row > col, row < col)
    tri_bf = (incl[0].astype(BF16), incl[1].astype(BF16))
    dirs = ((qf_ref, abf_ref, of_ref), (qb_ref, abb_ref, ob_ref))
    hk = DN_HEADS * DN_DK
    gate_lane = 2 * DN_HEADS
    n_double = int(math.log2(c)) - 1

    units = [(d, h) for d in range(2) for h in range(DN_HEADS)]

    def chunk_body(n, carry):
        gates = []
        for d in range(2):
            nn = n if d == 0 else nchunk - 1 - n
            r0 = pl.multiple_of(nn * c, c)
            ab = dirs[d][1][pl.ds(r0, c), :]
            beta = jax.nn.sigmoid(ab)
            g = -jnp.exp(pv_ref[0:1, :]) * _softplus(ab + pv_ref[1:2, :])
            gc = _split_dot(tri_bf[d], g)
            tot = gc[c - 1:c, :] if d == 0 else gc[0:1, :]
            gates.append(dict(r0=r0, beta=beta, gc=gc, gc_t=gc.T, e_gc=jnp.exp(gc),
                              e_kd=jnp.exp(tot - gc), e_tot=jnp.exp(tot)))

        st = []
        for d, h in units:
            ga = gates[d]
            x_ref = dirs[d][0]
            j = d * DN_HEADS + h
            gj = gate_lane + j
            rows = pl.ds(ga['r0'], c)
            q = x_ref[rows, h * DN_DK:(h + 1) * DN_DK]
            k = x_ref[rows, hk + h * DN_DK:hk + (h + 1) * DN_DK]
            v = x_ref[rows, 2 * hk + h * DN_DV:2 * hk + (h + 1) * DN_DV]
            kf = k.astype(F32)
            b_col = ga['beta'][:, j:j + 1]
            kb = kf * b_col
            eg_col = ga['e_gc'][:, gj:gj + 1]
            diff = ga['gc'][:, gj:gj + 1] - ga['gc_t'][gj:gj + 1, :]
            decay = jnp.exp(jnp.where(incl[d], diff, -jnp.inf))
            gq = _dot_nt(jnp.concatenate([kb.astype(BF16), q], axis=0), k)
            st.append(dict(
                d=d, h=h, j=j, rows=rows,
                nm=jnp.where(strict[d], -(gq[:c] * decay), 0.0),
                qk=(gq[c:] * decay).astype(BF16),
                xs=jnp.concatenate([v.astype(F32) * b_col, kb * eg_col], axis=1),
                qd=(q.astype(F32) * eg_col).astype(BF16),
                kd=(kf * ga['e_kd'][:, gj:gj + 1]).astype(BF16),
                e_tot=ga['e_tot'][:, gj:gj + 1]))

        for u in st:
            nm_bf = u['nm'].astype(BF16)
            u['r'] = u['nm']
            u['m'] = _dot(nm_bf, nm_bf)
        for _ in range(n_double - 1):
            for u in st:
                m_bf = u['m'].astype(BF16)
                rm = _dot(jnp.concatenate([u['r'].astype(BF16), m_bf], axis=0), m_bf)
                u['r'] = u['r'] + u['m'] + rm[:c]
                u['m'] = rm[c:]
        for u in st:
            u['r'] = u['r'] + u['m'] + _dot(u['r'].astype(BF16), u['m'].astype(BF16))
        for u in st:
            u['uw'] = u['xs'] + _dot(u['r'].astype(BF16), u['xs'].astype(BF16))
        for u in st:
            u['s'] = s_ref[u['j']]
            w_bf = u['uw'][:, DN_DV:].astype(BF16)
            u['wq'] = _dot(jnp.concatenate([w_bf, u['qd']], axis=0), u['s'].astype(BF16))
        for u in st:
            u['vn'] = (u['uw'][:, :DN_DV] - u['wq'][:c]).astype(BF16)
            s_ref[u['j']] = u['s'] * u['e_tot'] + _dot_tn(u['kd'], u['vn'])
        for u in st:
            o = u['wq'][c:] + _dot(u['qk'], u['vn'])
            dirs[u['d']][2][u['rows'], u['h'] * DN_DV:(u['h'] + 1) * DN_DV] = o
        return carry

    lax.fori_loop(0, nchunk, chunk_body, 0, unroll=True)


def _dn_scan_call(qkvn, ab, pvec):
    b, t_len, cols = qkvn.shape
    tt = min(TIME_TILE, t_len)
    nt = t_len // tt
    c = min(DN_CHUNK, tt)
    dv = DN_HEADS * DN_DV
    fwd = lambda n: pl.BlockSpec((None, tt, n), lambda b, i: (b, i, 0))
    bwd = lambda n: pl.BlockSpec((None, tt, n), lambda b, i: (b, nt - 1 - i, 0))
    return pl.pallas_call(
        functools.partial(_dn_scan_kernel, tt=tt, c=c),
        grid=(b, nt),
        in_specs=[fwd(cols), bwd(cols), fwd(LANES), bwd(LANES),
                  pl.BlockSpec(pvec.shape, lambda b, i: (0, 0))],
        out_specs=[fwd(dv), bwd(dv)],
        out_shape=[jax.ShapeDtypeStruct((b, t_len, dv), F32)] * 2,
        scratch_shapes=[pltpu.VMEM((2 * DN_HEADS, DN_DK, DN_DV), F32)],
        compiler_params=pltpu.CompilerParams(
            dimension_semantics=("parallel", "arbitrary"), vmem_limit_bytes=VMEM_LIMIT),
        name="dn_scan",
    )(qkvn, qkvn, ab, ab, pvec)


def _attn_kernel(q_ref, kp_ref, kc_ref, kn_ref, qg_ref, kg_ref, sink_ref, out_ref,
                 kd_ref, vd_ref, bias_ref, *, tq, w):
    i = pl.program_id(1)
    nblk = pl.num_programs(1) * (tq // w)
    d = AT_HEAD_DIM
    lane = lax.broadcasted_iota(jnp.int32, (1, 2 * d), 1)
    lo = lane < d

    @pl.when(i == 0)
    def _():
        qi = lax.broadcasted_iota(jnp.int32, (w, 3 * w), 0)
        kj = lax.broadcasted_iota(jnp.int32, (w, 3 * w), 1)
        dist = jnp.abs(qi - (kj - w))
        for variant in range(4):
            ok = dist <= w
            if variant & 1:
                ok = ok & (kj >= w)
            if variant & 2:
                ok = ok & (kj < 2 * w)
            for h in range(AT_HEADS):
                slope = 2.0 ** (-ALIBI_MAX * (h + 1) / AT_HEADS)
                bias_ref[variant * AT_HEADS + h] = jnp.where(ok, (-slope * LOG2E) * dist.astype(F32), -jnp.inf)

    def head_rms(x, g):
        sq = x * x
        s_lo = jnp.sum(jnp.where(lo, sq, 0.0), axis=-1, keepdims=True)
        s_hi = jnp.sum(jnp.where(lo, 0.0, sq), axis=-1, keepdims=True)
        ms = jnp.where(lo, s_lo, s_hi) * (1.0 / d)
        return x * lax.rsqrt(ms + EPS) * g

    def fill_kv(src_ref, r0, rows):
        x = src_ref[...]
        kn = head_rms(x[:, :2 * d].astype(F32), kg_ref[...])
        vf = x[:, 2 * d:].astype(F32)
        k_roll = pltpu.roll(kn, d, axis=1)
        v_roll = pltpu.roll(vf, d, axis=1)
        kd_ref[0, r0:r0 + rows, :] = jnp.where(lo, kn, k_roll).astype(BF16)
        kd_ref[1, r0:r0 + rows, :] = jnp.where(lo, k_roll, kn).astype(BF16)
        vd_ref[0, r0:r0 + rows, :] = jnp.where(lo, vf, v_roll).astype(BF16)
        vd_ref[1, r0:r0 + rows, :] = jnp.where(lo, v_roll, vf).astype(BF16)

    fill_kv(kp_ref, 0, w)
    fill_kv(kc_ref, w, tq)
    fill_kv(kn_ref, w + tq, w)

    group = AT_HEADS // AT_KV_HEADS
    nsub = tq // w
    units = [(jb, g) for jb in range(nsub) for g in range(AT_KV_HEADS)]

    def scores(jb, g):
        qn = head_rms(q_ref[jb * w:(jb + 1) * w, g * 2 * d:(g + 1) * 2 * d].astype(F32),
                      qg_ref[...]) * (d ** -0.5 * LOG2E)
        lhs = jnp.concatenate([jnp.where(lo, qn, 0.0), jnp.where(lo, 0.0, qn)], axis=0)
        return _dot_nt(lhs.astype(BF16), kd_ref[g, jb * w:jb * w + 3 * w, :])

    all_s = [scores(*u) for u in units]
    for idx, (jb, g) in enumerate(units):
        s = all_s[idx]
        n = i * nsub + jb
        variant = None
        if jb == 0:
            variant = (n == 0).astype(jnp.int32)
        if jb == nsub - 1:
            at_end = 2 * (n == nblk - 1).astype(jnp.int32)
            variant = at_end if variant is None else variant + at_end
        ps = []
        inv = []
        for hh in range(group):
            h = g * group + hh
            sink = sink_ref[h] * LOG2E
            bias = bias_ref[h] if variant is None else bias_ref[variant * AT_HEADS + h]
            sh = s[hh * w:(hh + 1) * w] + bias
            m = jnp.maximum(jnp.max(sh, axis=-1, keepdims=True), sink)
            p = jnp.exp2(sh - m)
            den = jnp.sum(p, axis=-1, keepdims=True) + jnp.exp2(sink - m)
            ps.append(p.astype(BF16))
            inv.append(1.0 / den)
        pv = _dot(jnp.concatenate(ps, axis=0), vd_ref[g, jb * w:jb * w + 3 * w, :])
        o = jnp.where(lo, pv[:w] * inv[0], pv[w:] * inv[1])
        out_ref[jb * w:(jb + 1) * w, g * 2 * d:(g + 1) * 2 * d] = o.astype(BF16)


def _attn_call(q, kv, q_g2, k_g2, sink):
    b, t_len, qc = q.shape
    w = AT_WINDOW
    tq = min(TIME_TILE, t_len)
    kvc = kv.shape[2]
    two_d = 2 * AT_HEAD_DIM
    return pl.pallas_call(
        functools.partial(_attn_kernel, tq=tq, w=w),
        grid=(b, t_len // tq),
        in_specs=[pl.BlockSpec((None, tq, qc), lambda b, i: (b, i, 0))]
        + _halo_specs(tq, w, t_len, kvc)
        + [pl.BlockSpec(q_g2.shape, lambda b, i: (0, 0)),
           pl.BlockSpec(k_g2.shape, lambda b, i: (0, 0)),
           pl.BlockSpec(memory_space=pltpu.SMEM)],
        out_specs=pl.BlockSpec((None, tq, qc), lambda b, i: (b, i, 0)),
        out_shape=jax.ShapeDtypeStruct((b, t_len, qc), BF16),
        scratch_shapes=[pltpu.VMEM((AT_KV_HEADS, tq + 2 * w, two_d), BF16),
                        pltpu.VMEM((AT_KV_HEADS, tq + 2 * w, two_d), BF16),
                        pltpu.VMEM((4 * AT_HEADS, w, 3 * w), F32)],
        compiler_params=pltpu.CompilerParams(
            dimension_semantics=("parallel", "arbitrary"), vmem_limit_bytes=VMEM_LIMIT),
        name="band_attn",
    )(q, kv, kv, kv, q_g2, k_g2, sink)


def _layer_params(l, ffn1_norm, ffn1_w_gate, ffn1_w_up, ffn1_w_down, mix_norm, w_in,
                  dn_conv, dn_a_log, dn_dt_bias, dn_out_norm, cv_dw, cv_dw_bias, cv_ln_g, cv_ln_b,
                  at_q_norm, at_k_norm, at_sink, w_out, ffn2_norm, ffn2_w_gate, ffn2_w_up,
                  ffn2_w_down, final_norm):
    row = lambda a: a[l].reshape(1, -1).astype(F32)
    bf = lambda a: a.astype(BF16)
    hk = DN_HEADS * DN_DK
    hv = DN_HEADS * DN_DV
    cv_ch = cv_dw.shape[2]
    sizes = (2 * hk + hv, hv, 2 * DN_HEADS, 2 * DN_HEADS, 2 * cv_ch,
             AT_HEADS * AT_HEAD_DIM, 2 * AT_KV_HEADS * AT_HEAD_DIM)
    offs = [0]
    for s in sizes:
        offs.append(offs[-1] + s)
    wi = w_in[l]
    piece = lambda n: wi[:, offs[n]:offs[n + 1]]
    n_gate = 4 * DN_HEADS
    w_ab = jnp.pad(jnp.concatenate([piece(2), piece(3)], axis=1), ((0, 0), (0, LANES - n_gate)))
    pad_gate = lambda a: jnp.pad(a.reshape(1, -1).astype(F32),
                                 ((0, 0), (2 * DN_HEADS, LANES - n_gate)))
    pvec = jnp.concatenate([pad_gate(dn_a_log[l]), pad_gate(dn_dt_bias[l])], axis=0)
    two = lambda a: jnp.concatenate([row(a), row(a)], axis=1)
    return dict(
        g1=row(ffn1_norm), wg1=bf(ffn1_w_gate[l]), wu1=bf(ffn1_w_up[l]), wd1=bf(ffn1_w_down[l]),
        gm=row(mix_norm), w_qkv=bf(piece(0)), w_z=bf(piece(1)), w_ab=bf(w_ab), w_glu=bf(piece(4)),
        w_q=bf(piece(5)), w_kv=bf(piece(6)),
        dn_conv=dn_conv[l].astype(F32), pvec=pvec, og=row(dn_out_norm),
        cv_dw=cv_dw[l].astype(F32), cv_b=row(cv_dw_bias), cv_g=row(cv_ln_g), cv_bb=row(cv_ln_b),
        q_g2=two(at_q_norm), k_g2=two(at_k_norm), sink=at_sink[l].astype(F32),
        w_out=bf(w_out[l]), g2=row(ffn2_norm), wg2=bf(ffn2_w_gate[l]), wu2=bf(ffn2_w_up[l]),
        wd2=bf(ffn2_w_down[l]), gf=row(final_norm))


def _layer(x, b, t_len, p):
    x1, qkv, z, glu, q_at, kv_at, ab = _ffn_in_call(x, p)
    seq = lambda a: a.reshape(b, t_len, a.shape[-1])
    flat = lambda a: a.reshape(b * t_len, a.shape[-1])
    qkvn = _dn_prep_call(seq(qkv), p['dn_conv'])
    o_f, o_b = _dn_scan_call(qkvn, seq(ab), p['pvec'])
    o_cv = _conf_call(seq(glu), p['cv_dw'], p['cv_b'], p['cv_g'], p['cv_bb'])
    o_at = _attn_call(seq(q_at), seq(kv_at), p['q_g2'], p['k_g2'], p['sink'])
    return _out_ffn_call(x1, flat(o_f), flat(o_b), z, flat(o_cv), flat(o_at), p)


def kernel(x_prompt, x_sample, ffn1_norm, ffn1_w_gate, ffn1_w_up, ffn1_w_down, mix_norm, w_in, dn_conv, dn_a_log, dn_dt_bias, dn_out_norm, cv_dw, cv_dw_bias, cv_ln_g, cv_ln_b, at_q_norm, at_k_norm, at_sink, w_out, ffn2_norm, ffn2_w_gate, ffn2_w_up, ffn2_w_down, final_norm):
    params = (ffn1_norm, ffn1_w_gate, ffn1_w_up, ffn1_w_down, mix_norm, w_in,
              dn_conv, dn_a_log, dn_dt_bias, dn_out_norm, cv_dw, cv_dw_bias, cv_ln_g, cv_ln_b,
              at_q_norm, at_k_norm, at_sink, w_out, ffn2_norm, ffn2_w_gate, ffn2_w_up, ffn2_w_down,
              final_norm)
    depth = ffn1_norm.shape[0]
    layers = [_layer_params(l, *params) for l in range(depth)]

    def trunk(x):
        b, t_len, d = x.shape
        y = x.reshape(b * t_len, d)
        for p in layers:
            y = _layer(y, b, t_len, p)
        return y.reshape(b, t_len, d)

    return (trunk(x_prompt), trunk(x_sample))
```

```python
import functools

import jax
import jax.numpy as jnp
from jax import lax
from jax.experimental import pallas as pl
from jax.experimental.pallas import tpu as pltpu

F32 = jnp.float32
BF16 = jnp.bfloat16
EPS = 1e-6

DN_HEADS = 4
DN_DK = 128
DN_DV = 128
DN_CONV = 5
CV_KERNEL = 31
AT_HEADS = 4
AT_KV_HEADS = 2
AT_HEAD_DIM = 64
AT_WINDOW = 128
ALIBI_MAX = 8.0
LOG2E = 1.4426950408889634

LANES = 128
BF16_SUBLANES = 16
F32_SUBLANES = 8

ROW_TILE = 512
FF_CHUNK = 256
TIME_TILE = 512
ATTN_TILE = 2048
ATTN_AHEAD = 16
DN_CHUNK = 128
CONV_ROWS = 64
VMEM_LIMIT = 56 * 1024 * 1024


def _dot(a, b):
    return jnp.dot(a, b, preferred_element_type=F32)


def _dot_nt(a, b):
    return lax.dot_general(a, b, (((1,), (1,)), ((), ())), preferred_element_type=F32)


def _dot_tn(a, b):
    return lax.dot_general(a, b, (((0,), (0,)), ((), ())), preferred_element_type=F32)


def _rms(x, g):
    return x * lax.rsqrt(jnp.mean(x * x, axis=-1, keepdims=True) + EPS) * g


def _silu(x):
    return x * jax.nn.sigmoid(x)


def _softplus(x):
    return jnp.maximum(x, 0.0) + jnp.log1p(jnp.exp(-jnp.abs(x)))


def _const_spec(shape):
    nd = len(shape)
    return pl.BlockSpec(shape, lambda *_: (0,) * nd, pipeline_mode=pl.Buffered(1))


def _swiglu(h_bf, wg_ref, wu_ref, wd_ref, act_ref):
    d_ff = wg_ref.shape[1]
    for c in range(0, d_ff, FF_CHUNK):
        gate = _dot(h_bf, wg_ref[:, c:c + FF_CHUNK])
        up = _dot(h_bf, wu_ref[:, c:c + FF_CHUNK])
        act_ref[:, c:c + FF_CHUNK] = (_silu(gate) * up).astype(BF16)
    return _dot(act_ref[...], wd_ref[...])


def _ffn_in_kernel(x_ref, g1_ref, wg_ref, wu_ref, wd_ref, gm_ref,
                   wqkv_ref, wz_ref, wglu_ref, wq_ref, wkv_ref, wab_ref,
                   x1_ref, qkv_ref, z_ref, glu_ref, q_ref, kv_ref, ab_ref,
                   act_ref):
    x = x_ref[...]
    h = _rms(x, g1_ref[...]).astype(BF16)
    x1 = x + 0.5 * _swiglu(h, wg_ref, wu_ref, wd_ref, act_ref)
    x1_ref[...] = x1
    h2 = _rms(x1, gm_ref[...]).astype(BF16)
    qkv_ref[...] = _dot(h2, wqkv_ref[...]).astype(BF16)
    z_ref[...] = _dot(h2, wz_ref[...]).astype(BF16)
    glu_ref[...] = _dot(h2, wglu_ref[...]).astype(BF16)
    q_ref[...] = _dot(h2, wq_ref[...]).astype(BF16)
    kv_ref[...] = _dot(h2, wkv_ref[...]).astype(BF16)
    ab_ref[...] = _dot(h2, wab_ref[...])


def _ffn_in_call(x, p):
    rows, d = x.shape
    tm = min(ROW_TILE, rows)
    d_ff = p['wg1'].shape[1]
    weights = [p['g1'], p['wg1'], p['wu1'], p['wd1'], p['gm'],
               p['w_qkv'], p['w_z'], p['w_glu'], p['w_q'], p['w_kv'], p['w_ab']]
    out_cols = [(d, F32)] + [(w.shape[1], BF16) for w in weights[5:10]] + [(LANES, F32)]
    row_spec = lambda n: pl.BlockSpec((tm, n), lambda i: (i, 0))
    return pl.pallas_call(
        _ffn_in_kernel,
        grid=(rows // tm,),
        in_specs=[row_spec(d)] + [_const_spec(w.shape) for w in weights],
        out_specs=[row_spec(n) for n, _ in out_cols],
        out_shape=[jax.ShapeDtypeStruct((rows, n), dt) for n, dt in out_cols],
        scratch_shapes=[pltpu.VMEM((tm, d_ff), BF16)],
        compiler_params=pltpu.CompilerParams(
            dimension_semantics=("parallel",), vmem_limit_bytes=VMEM_LIMIT),
        name="ffn_in",
    )(x, *weights)


def _out_ffn_kernel(x1_ref, of_ref, ob_ref, z_ref, ocv_ref, oat_ref, og_ref, wo_ref,
                    g2_ref, wg_ref, wu_ref, wd_ref, gf_ref, y_ref, mix_ref, act_ref):
    dn_cols = of_ref.shape[1]
    for h in range(dn_cols // DN_DV):
        sl = slice(h * DN_DV, (h + 1) * DN_DV)
        o = _rms(of_ref[:, sl] + ob_ref[:, sl], og_ref[...])
        mix_ref[:, sl] = (o * _silu(z_ref[:, sl].astype(F32))).astype(BF16)
    cv_cols = ocv_ref.shape[1]
    mix_ref[:, dn_cols:dn_cols + cv_cols] = ocv_ref[...]
    mix_ref[:, dn_cols + cv_cols:] = oat_ref[...]
    x2 = x1_ref[...] + _dot(mix_ref[...], wo_ref[...])
    h2 = _rms(x2, g2_ref[...]).astype(BF16)
    x3 = x2 + 0.5 * _swiglu(h2, wg_ref, wu_ref, wd_ref, act_ref)
    y_ref[...] = _rms(x3, gf_ref[...])


def _out_ffn_call(x1, o_f, o_b, z, o_cv, o_at, p):
    rows, d = x1.shape
    tm = min(ROW_TILE, rows)
    d_ff = p['wg2'].shape[1]
    weights = [p['og'], p['w_out'], p['g2'], p['wg2'], p['wu2'], p['wd2'], p['gf']]
    acts = [x1, o_f, o_b, z, o_cv, o_at]
    row_spec = lambda n: pl.BlockSpec((tm, n), lambda i: (i, 0))
    return pl.pallas_call(
        _out_ffn_kernel,
        grid=(rows // tm,),
        in_specs=[row_spec(a.shape[1]) for a in acts] + [_const_spec(w.shape) for w in weights],
        out_specs=row_spec(d),
        out_shape=jax.ShapeDtypeStruct((rows, d), F32),
        scratch_shapes=[pltpu.VMEM((tm, d), BF16), pltpu.VMEM((tm, d_ff), BF16)],
        compiler_params=pltpu.CompilerParams(
            dimension_semantics=("parallel",), vmem_limit_bytes=VMEM_LIMIT),
        name="out_ffn",
    )(*acts, *weights)


def _halo_specs(tt, halo, t_len, cols):
    per = tt // halo
    last = t_len // halo - 1
    prev = pl.BlockSpec((None, halo, cols), lambda b, i: (b, jnp.maximum(i * per - 1, 0), 0))
    cur = pl.BlockSpec((None, tt, cols), lambda b, i: (b, i, 0))
    nxt = pl.BlockSpec((None, halo, cols), lambda b, i: (b, jnp.minimum((i + 1) * per, last), 0))
    return [prev, cur, nxt]


def _fill_with_halo(dst_ref, prev, cur, nxt, halo, tt):
    i = pl.program_id(1)
    last = pl.num_programs(1) - 1
    dst_ref[0:halo, :] = jnp.where(i > 0, prev, jnp.zeros_like(prev))
    dst_ref[halo:halo + tt, :] = cur
    dst_ref[halo + tt:halo + tt + halo, :] = jnp.where(i < last, nxt, jnp.zeros_like(nxt))


def _dn_prep_kernel(prev_ref, cur_ref, next_ref, w_ref, out_ref, xw_ref, *, tt, halo):
    _fill_with_halo(xw_ref, prev_ref[...], cur_ref[...], next_ref[...], halo, tt)
    cols = cur_ref.shape[1]
    left = (DN_CONV - 1) // 2
    qk_groups = 2 * DN_HEADS * DN_DK // LANES
    q_groups = DN_HEADS * DN_DK // LANES
    win_rows = CONV_ROWS + 2 * halo
    out_row = lax.broadcasted_iota(jnp.int32, (DN_CONV * CONV_ROWS, win_rows), 0)
    src_row = lax.broadcasted_iota(jnp.int32, (DN_CONV * CONV_ROWS, win_rows), 1)
    tap = out_row // CONV_ROWS
    shift = (src_row == (out_row - tap * CONV_ROWS) + (halo - left) + tap).astype(BF16)
    group = 2 * LANES
    for rb in range(tt // CONV_ROWS):
        r0 = rb * CONV_ROWS
        for j in range(cols // group):
            gs = slice(j * group, (j + 1) * group)
            taps = _dot(shift, xw_ref[r0:r0 + win_rows, gs])
            acc = None
            for k in range(DN_CONV):
                term = taps[k * CONV_ROWS:(k + 1) * CONV_ROWS, :] * w_ref[k:k + 1, gs]
                acc = term if acc is None else acc + term
            y = _silu(acc)
            for e in range(group // LANES):
                jj = j * (group // LANES) + e
                ye = y[:, e * LANES:(e + 1) * LANES]
                if jj < qk_groups:
                    ye = ye * lax.rsqrt(jnp.sum(ye * ye, axis=-1, keepdims=True) + EPS)
                if jj < q_groups:
                    ye = ye * (DN_DK ** -0.5)
                out_ref[r0:r0 + CONV_ROWS, jj * LANES:(jj + 1) * LANES] = ye.astype(BF16)


def _dn_prep_call(qkv, conv_w):
    b, t_len, cols = qkv.shape
    tt = min(TIME_TILE, t_len)
    halo = BF16_SUBLANES
    return pl.pallas_call(
        functools.partial(_dn_prep_kernel, tt=tt, halo=halo),
        grid=(b, t_len // tt),
        in_specs=_halo_specs(tt, halo, t_len, cols) + [pl.BlockSpec(conv_w.shape, lambda b, i: (0, 0))],
        out_specs=pl.BlockSpec((None, tt, cols), lambda b, i: (b, i, 0)),
        out_shape=jax.ShapeDtypeStruct((b, t_len, cols), BF16),
        scratch_shapes=[pltpu.VMEM((tt + 2 * halo, cols), BF16)],
        compiler_params=pltpu.CompilerParams(
            dimension_semantics=("parallel", "parallel"), vmem_limit_bytes=VMEM_LIMIT),
        name="dn_prep",
    )(qkv, qkv, qkv, conv_w)


def _conf_kernel(prev_ref, cur_ref, next_ref, dw_ref, b_ref, g_ref, bb_ref, out_ref, sh_ref,
                 *, tt, halo):
    ch = out_ref.shape[1]

    def glu(x_ref):
        x = x_ref[...].astype(F32)
        return x[:, :ch] * jax.nn.sigmoid(x[:, ch:])

    _fill_with_halo(sh_ref.at[0], glu(prev_ref), glu(cur_ref), glu(next_ref), halo, tt)
    left = (CV_KERNEL - 1) // 2
    span = CONV_ROWS + 2 * halo - F32_SUBLANES
    for rb in range(tt // CONV_ROWS):
        r0 = rb * CONV_ROWS
        win = sh_ref[0, r0:r0 + CONV_ROWS + 2 * halo, :]
        for r in range(1, F32_SUBLANES):
            sh_ref[r, r0:r0 + span, :] = win[r:r + span, :]
        acc = None
        for k in range(CV_KERNEL):
            off = halo - left + k
            res = off % F32_SUBLANES
            base = r0 + off - res
            term = sh_ref[res, base:base + CONV_ROWS, :] * dw_ref[k:k + 1, :]
            acc = term if acc is None else acc + term
        acc = acc + b_ref[...]
        xc = acc - jnp.mean(acc, axis=-1, keepdims=True)
        var = jnp.mean(xc * xc, axis=-1, keepdims=True)
        y = xc * lax.rsqrt(var + EPS) * g_ref[...] + bb_ref[...]
        out_ref[r0:r0 + CONV_ROWS, :] = _silu(y).astype(BF16)


def _conf_call(glu_in, dw, dw_b, ln_g, ln_b):
    b, t_len, cols = glu_in.shape
    ch = cols // 2
    tt = min(TIME_TILE, t_len)
    halo = BF16_SUBLANES
    small = [dw, dw_b, ln_g, ln_b]
    return pl.pallas_call(
        functools.partial(_conf_kernel, tt=tt, halo=halo),
        grid=(b, t_len // tt),
        in_specs=_halo_specs(tt, halo, t_len, cols)
        + [pl.BlockSpec(a.shape, lambda b, i: (0, 0)) for a in small],
        out_specs=pl.BlockSpec((None, tt, ch), lambda b, i: (b, i, 0)),
        out_shape=jax.ShapeDtypeStruct((b, t_len, ch), BF16),
        scratch_shapes=[pltpu.VMEM((F32_SUBLANES, tt + 2 * halo, ch), F32)],
        compiler_params=pltpu.CompilerParams(
            dimension_semantics=("parallel", "parallel"), vmem_limit_bytes=VMEM_LIMIT),
        name="conformer",
    )(glu_in, glu_in, glu_in, *small)


def _split_dot(m_bf, x):
    hi = x.astype(BF16)
    r1 = x - hi.astype(F32)
    mid = r1.astype(BF16)
    lo = (r1 - mid.astype(F32)).astype(BF16)
    return _dot(m_bf, hi) + _dot(m_bf, mid) + _dot(m_bf, lo)


def _dn_scan_kernel(qf_ref, qb_ref, abf_ref, abb_ref, pv_ref, of_ref, ob_ref, s_ref, *, tt, c):
    @pl.when(pl.program_id(1) == 0)
    def _():
        s_ref[...] = jnp.zeros_like(s_ref)

    nchunk = tt // c
    row = lax.broadcasted_iota(jnp.int32, (c, c), 0)
    col = lax.broadcasted_iota(jnp.int32, (c, c), 1)
    incl = (row >= col, row <= col)
    strict = (row > col, row < col)
    tri_bf = (incl[0].astype(BF16), incl[1].astype(BF16))
    dirs = ((qf_ref, abf_ref, of_ref), (qb_ref, abb_ref, ob_ref))
    hk = DN_HEADS * DN_DK
    gate_lane = 2 * DN_HEADS

    units = [(d, h) for d in range(2) for h in range(DN_HEADS)]

    levels = []
    shift = 0
    while (1 << shift) < c:
        bi = row >> shift
        bj = col >> shift
        levels.append((((bi & 1) == 1) & (bj == bi - 1), ((bj & 1) == 1) & (bi == bj - 1)))
        shift += 1

    def prepare(n):
        gates = []
        for d in range(2):
            nn = n if d == 0 else nchunk - 1 - n
            r0 = pl.multiple_of(nn * c, c)
            ab = dirs[d][1][pl.ds(r0, c), :]
            beta = jax.nn.sigmoid(ab)
            g = -jnp.exp(pv_ref[0:1, :]) * _softplus(ab + pv_ref[1:2, :])
            gc = _split_dot(tri_bf[d], g)
            tot = gc[c - 1:c, :] if d == 0 else gc[0:1, :]
            gates.append(dict(r0=r0, beta=beta, gc=gc, gc_t=gc.T, e_gc=jnp.exp(gc),
                              e_kd=jnp.exp(tot - gc), e_tot=jnp.exp(tot)))
        st = []
        for d, h in units:
            ga = gates[d]
            x_ref = dirs[d][0]
            j = d * DN_HEADS + h
            gj = gate_lane + j
            rows = pl.ds(ga['r0'], c)
            q = x_ref[rows, h * DN_DK:(h + 1) * DN_DK]
            k = x_ref[rows, hk + h * DN_DK:hk + (h + 1) * DN_DK]
            v = x_ref[rows, 2 * hk + h * DN_DV:2 * hk + (h + 1) * DN_DV]
            kf = k.astype(F32)
            b_col = ga['beta'][:, j:j + 1]
            kb = kf * b_col
            eg_col = ga['e_gc'][:, gj:gj + 1]
            diff = ga['gc'][:, gj:gj + 1] - ga['gc_t'][gj:gj + 1, :]
            decay = jnp.exp(jnp.where(incl[d], diff, -jnp.inf))
            gq = _dot_nt(jnp.concatenate([kb.astype(BF16), q], axis=0), k)
            st.append(dict(
                d=d, h=h, j=j, rows=rows,
                a=jnp.where(strict[d], gq[:c] * decay, 0.0),
                qk=(gq[c:] * decay).astype(BF16),
                xs=jnp.concatenate([v.astype(F32) * b_col, kb * eg_col], axis=1),
                qd=(q.astype(F32) * eg_col).astype(BF16),
                kd=(kf * ga['e_kd'][:, gj:gj + 1]).astype(BF16),
                e_tot=ga['e_tot'][:, gj:gj + 1]))
        return st

    def solve(st):
        for u in st:
            u['r'] = -jnp.where(levels[0][u['d']], u['a'], 0.0)
        for masks in levels[1:]:
            for u in st:
                e = jnp.where(masks[u['d']], u['a'], 0.0)
                u['x'] = e + _dot(u['r'].astype(BF16), e.astype(BF16))
            for u in st:
                u['r'] = u['r'] - u['x'] - _dot(u['x'].astype(BF16), u['r'].astype(BF16))
        for u in st:
            u['uw'] = u['xs'] + _dot(u['r'].astype(BF16), u['xs'].astype(BF16))

    def advance(st):
        for u in st:
            u['s'] = s_ref[u['j']]
            w_bf = u['uw'][:, DN_DV:].astype(BF16)
            u['wq'] = _dot(jnp.concatenate([w_bf, u['qd']], axis=0), u['s'].astype(BF16))
        for u in st:
            u['vn'] = (u['uw'][:, :DN_DV] - u['wq'][:c]).astype(BF16)
            s_ref[u['j']] = u['s'] * u['e_tot'] + _dot_tn(u['kd'], u['vn'])
        for u in st:
            o = u['wq'][c:] + _dot(u['qk'], u['vn'])
            dirs[u['d']][2][u['rows'], u['h'] * DN_DV:(u['h'] + 1) * DN_DV] = o

    group = 2 if nchunk % 2 == 0 else 1

    def chunk_body(n, carry):
        sts = [prepare(n * group + e) for e in range(group)]
        solve([u for st in sts for u in st])
        for st in sts:
            advance(st)
        return carry

    lax.fori_loop(0, nchunk // group, chunk_body, 0, unroll=True)


def _dn_scan_call(qkvn, ab, pvec):
    b, t_len, cols = qkvn.shape
    tt = min(TIME_TILE, t_len)
    nt = t_len // tt
    c = min(DN_CHUNK, tt)
    dv = DN_HEADS * DN_DV
    fwd = lambda n: pl.BlockSpec((None, tt, n), lambda b, i: (b, i, 0))
    bwd = lambda n: pl.BlockSpec((None, tt, n), lambda b, i: (b, nt - 1 - i, 0))
    return pl.pallas_call(
        functools.partial(_dn_scan_kernel, tt=tt, c=c),
        grid=(b, nt),
        in_specs=[fwd(cols), bwd(cols), fwd(LANES), bwd(LANES),
                  pl.BlockSpec(pvec.shape, lambda b, i: (0, 0))],
        out_specs=[fwd(dv), bwd(dv)],
        out_shape=[jax.ShapeDtypeStruct((b, t_len, dv), F32)] * 2,
        scratch_shapes=[pltpu.VMEM((2 * DN_HEADS, DN_DK, DN_DV), F32)],
        compiler_params=pltpu.CompilerParams(
            dimension_semantics=("parallel", "arbitrary"), vmem_limit_bytes=VMEM_LIMIT),
        name="dn_scan",
    )(qkvn, qkvn, ab, ab, pvec)


def _attn_kernel(q_ref, kp_ref, kc_ref, kn_ref, qg_ref, kg_ref, sink_ref, out_ref,
                 kd_ref, vd_ref, bias_ref, *, tq, w):
    i = pl.program_id(1)
    nblk = pl.num_programs(1) * (tq // w)
    d = AT_HEAD_DIM
    lane = lax.broadcasted_iota(jnp.int32, (1, 2 * d), 1)
    lo = lane < d

    @pl.when(i == 0)
    def _():
        qi = lax.broadcasted_iota(jnp.int32, (w, 3 * w), 0)
        kj = lax.broadcasted_iota(jnp.int32, (w, 3 * w), 1)
        dist = jnp.abs(qi - (kj - w))
        for variant in range(4):
            ok = dist <= w
            if variant & 1:
                ok = ok & (kj >= w)
            if variant & 2:
                ok = ok & (kj < 2 * w)
            for h in range(AT_HEADS):
                slope = 2.0 ** (-ALIBI_MAX * (h + 1) / AT_HEADS)
                bias_ref[variant * AT_HEADS + h] = jnp.where(ok, (-slope * LOG2E) * dist.astype(F32), -jnp.inf)

    def head_rms(x, g):
        sq = x * x
        s_lo = jnp.sum(jnp.where(lo, sq, 0.0), axis=-1, keepdims=True)
        s_hi = jnp.sum(jnp.where(lo, 0.0, sq), axis=-1, keepdims=True)
        ms = jnp.where(lo, s_lo, s_hi) * (1.0 / d)
        return x * lax.rsqrt(ms + EPS) * g

    def fill_kv(src_ref, r0, rows):
        x = src_ref[...]
        kn = head_rms(x[:, :2 * d].astype(F32), kg_ref[...])
        vf = x[:, 2 * d:].astype(F32)
        k_roll = pltpu.roll(kn, d, axis=1)
        v_roll = pltpu.roll(vf, d, axis=1)
        kd_ref[0, r0:r0 + rows, :] = jnp.where(lo, kn, k_roll).astype(BF16)
        kd_ref[1, r0:r0 + rows, :] = jnp.where(lo, k_roll, kn).astype(BF16)
        vd_ref[0, r0:r0 + rows, :] = jnp.where(lo, vf, v_roll).astype(BF16)
        vd_ref[1, r0:r0 + rows, :] = jnp.where(lo, v_roll, vf).astype(BF16)

    fill_kv(kp_ref, 0, w)
    fill_kv(kc_ref, w, tq)
    fill_kv(kn_ref, w + tq, w)

    group = AT_HEADS // AT_KV_HEADS
    nsub = tq // w
    units = [(jb, g) for jb in range(nsub) for g in range(AT_KV_HEADS)]

    def scores(jb, g):
        qn = head_rms(q_ref[jb * w:(jb + 1) * w, g * 2 * d:(g + 1) * 2 * d].astype(F32),
                      qg_ref[...]) * (d ** -0.5 * LOG2E)
        lhs = jnp.concatenate([jnp.where(lo, qn, 0.0), jnp.where(lo, 0.0, qn)], axis=0)
        return _dot_nt(lhs.astype(BF16), kd_ref[g, jb * w:jb * w + 3 * w, :])

    ready = {}
    for idx, (jb, g) in enumerate(units):
        if idx % ATTN_AHEAD == 0:
            for nxt in range(idx, min(idx + ATTN_AHEAD, len(units))):
                ready[nxt] = scores(*units[nxt])
        s = ready.pop(idx)
        n = i * nsub + jb
        variant = None
        if jb == 0:
            variant = (n == 0).astype(jnp.int32)
        if jb == nsub - 1:
            at_end = 2 * (n == nblk - 1).astype(jnp.int32)
            variant = at_end if variant is None else variant + at_end
        ps = []
        inv = []
        for hh in range(group):
            h = g * group + hh
            sink = sink_ref[h] * LOG2E
            bias = bias_ref[h] if variant is None else bias_ref[variant * AT_HEADS + h]
            sh = s[hh * w:(hh + 1) * w] + bias
            m = jnp.maximum(jnp.max(sh, axis=-1, keepdims=True), sink)
            p = jnp.exp2(sh - m)
            den = jnp.sum(p, axis=-1, keepdims=True) + jnp.exp2(sink - m)
            ps.append(p.astype(BF16))
            inv.append(1.0 / den)
        pv = _dot(jnp.concatenate(ps, axis=0), vd_ref[g, jb * w:jb * w + 3 * w, :])
        o = jnp.where(lo, pv[:w] * inv[0], pv[w:] * inv[1])
        out_ref[jb * w:(jb + 1) * w, g * 2 * d:(g + 1) * 2 * d] = o.astype(BF16)


def _attn_call(q, kv, q_g2, k_g2, sink):
    b, t_len, qc = q.shape
    w = AT_WINDOW
    tq = min(ATTN_TILE, t_len)
    kvc = kv.shape[2]
    two_d = 2 * AT_HEAD_DIM
    return pl.pallas_call(
        functools.partial(_attn_kernel, tq=tq, w=w),
        grid=(b, t_len // tq),
        in_specs=[pl.BlockSpec((None, tq, qc), lambda b, i: (b, i, 0))]
        + _halo_specs(tq, w, t_len, kvc)
        + [pl.BlockSpec(q_g2.shape, lambda b, i: (0, 0)),
           pl.BlockSpec(k_g2.shape, lambda b, i: (0, 0)),
           pl.BlockSpec(memory_space=pltpu.SMEM)],
        out_specs=pl.BlockSpec((None, tq, qc), lambda b, i: (b, i, 0)),
        out_shape=jax.ShapeDtypeStruct((b, t_len, qc), BF16),
        scratch_shapes=[pltpu.VMEM((AT_KV_HEADS, tq + 2 * w, two_d), BF16),
                        pltpu.VMEM((AT_KV_HEADS, tq + 2 * w, two_d), BF16),
                        pltpu.VMEM((4 * AT_HEADS, w, 3 * w), F32)],
        compiler_params=pltpu.CompilerParams(
            dimension_semantics=("parallel", "arbitrary"), vmem_limit_bytes=VMEM_LIMIT),
        name="band_attn",
    )(q, kv, kv, kv, q_g2, k_g2, sink)


def _layer_params(l, ffn1_norm, ffn1_w_gate, ffn1_w_up, ffn1_w_down, mix_norm, w_in,
                  dn_conv, dn_a_log, dn_dt_bias, dn_out_norm, cv_dw, cv_dw_bias, cv_ln_g, cv_ln_b,
                  at_q_norm, at_k_norm, at_sink, w_out, ffn2_norm, ffn2_w_gate, ffn2_w_up,
                  ffn2_w_down, final_norm):
    row = lambda a: a[l].reshape(1, -1).astype(F32)
    bf = lambda a: a.astype(BF16)
    hk = DN_HEADS * DN_DK
    hv = DN_HEADS * DN_DV
    cv_ch = cv_dw.shape[2]
    sizes = (2 * hk + hv, hv, 2 * DN_HEADS, 2 * DN_HEADS, 2 * cv_ch,
             AT_HEADS * AT_HEAD_DIM, 2 * AT_KV_HEADS * AT_HEAD_DIM)
    offs = [0]
    for s in sizes:
        offs.append(offs[-1] + s)
    wi = w_in[l]
    piece = lambda n: wi[:, offs[n]:offs[n + 1]]
    n_gate = 4 * DN_HEADS
    w_ab = jnp.pad(jnp.concatenate([piece(2), piece(3)], axis=1), ((0, 0), (0, LANES - n_gate)))
    pad_gate = lambda a: jnp.pad(a.reshape(1, -1).astype(F32),
                                 ((0, 0), (2 * DN_HEADS, LANES - n_gate)))
    pvec = jnp.concatenate([pad_gate(dn_a_log[l]), pad_gate(dn_dt_bias[l])], axis=0)
    two = lambda a: jnp.concatenate([row(a), row(a)], axis=1)
    return dict(
        g1=row(ffn1_norm), wg1=bf(ffn1_w_gate[l]), wu1=bf(ffn1_w_up[l]), wd1=bf(ffn1_w_down[l]),
        gm=row(mix_norm), w_qkv=bf(piece(0)), w_z=bf(piece(1)), w_ab=bf(w_ab), w_glu=bf(piece(4)),
        w_q=bf(piece(5)), w_kv=bf(piece(6)),
        dn_conv=dn_conv[l].astype(F32), pvec=pvec, og=row(dn_out_norm),
        cv_dw=cv_dw[l].astype(F32), cv_b=row(cv_dw_bias), cv_g=row(cv_ln_g), cv_bb=row(cv_ln_b),
        q_g2=two(at_q_norm), k_g2=two(at_k_norm), sink=at_sink[l].astype(F32),
        w_out=bf(w_out[l]), g2=row(ffn2_norm), wg2=bf(ffn2_w_gate[l]), wu2=bf(ffn2_w_up[l]),
        wd2=bf(ffn2_w_down[l]), gf=row(final_norm))


def _layer(x, b, t_len, p):
    x1, qkv, z, glu, q_at, kv_at, ab = _ffn_in_call(x, p)
    seq = lambda a: a.reshape(b, t_len, a.shape[-1])
    flat = lambda a: a.reshape(b * t_len, a.shape[-1])
    qkvn = _dn_prep_call(seq(qkv), p['dn_conv'])
    o_f, o_b = _dn_scan_call(qkvn, seq(ab), p['pvec'])
    o_cv = _conf_call(seq(glu), p['cv_dw'], p['cv_b'], p['cv_g'], p['cv_bb'])
    o_at = _attn_call(seq(q_at), seq(kv_at), p['q_g2'], p['k_g2'], p['sink'])
    return _out_ffn_call(x1, flat(o_f), flat(o_b), z, flat(o_cv), flat(o_at), p)


def kernel(x_prompt, x_sample, ffn1_norm, ffn1_w_gate, ffn1_w_up, ffn1_w_down, mix_norm, w_in, dn_conv, dn_a_log, dn_dt_bias, dn_out_norm, cv_dw, cv_dw_bias, cv_ln_g, cv_ln_b, at_q_norm, at_k_norm, at_sink, w_out, ffn2_norm, ffn2_w_gate, ffn2_w_up, ffn2_w_down, final_norm):
    params = (ffn1_norm, ffn1_w_gate, ffn1_w_up, ffn1_w_down, mix_norm, w_in,
              dn_conv, dn_a_log, dn_dt_bias, dn_out_norm, cv_dw, cv_dw_bias, cv_ln_g, cv_ln_b,
              at_q_norm, at_k_norm, at_sink, w_out, ffn2_norm, ffn2_w_gate, ffn2_w_up, ffn2_w_down,
              final_norm)
    depth = ffn1_norm.shape[0]
    layers = [_layer_params(l, *params) for l in range(depth)]

    def trunk(x):
        b, t_len, d = x.shape
        y = x.reshape(b * t_len, d)
        for p in layers:
            y = _layer(y, b, t_len, p)
        return y.reshape(b, t_len, d)

    return (trunk(x_prompt), trunk(x_sample))
```

```python
import functools

import jax
import jax.numpy as jnp
from jax import lax
from jax.experimental import pallas as pl
from jax.experimental.pallas import tpu as pltpu

F32 = jnp.float32
BF16 = jnp.bfloat16
EPS = 1e-6

DN_HEADS = 4
DN_DK = 128
DN_DV = 128
DN_CONV = 5
CV_KERNEL = 31
AT_HEADS = 4
AT_KV_HEADS = 2
AT_HEAD_DIM = 64
AT_WINDOW = 128
ALIBI_MAX = 8.0
LOG2E = 1.4426950408889634

LANES = 128
BF16_SUBLANES = 16
F32_SUBLANES = 8

ROW_TILE = 512
FF_CHUNK = 256
TIME_TILE = 512
ATTN_TILE = 4096
ATTN_AHEAD = 16
DN_CHUNK = 128
CONV_ROWS = 64
VMEM_LIMIT = 56 * 1024 * 1024


def _dot(a, b):
    return jnp.dot(a, b, preferred_element_type=F32)


def _dot_nt(a, b):
    return lax.dot_general(a, b, (((1,), (1,)), ((), ())), preferred_element_type=F32)


def _dot_tn(a, b):
    return lax.dot_general(a, b, (((0,), (0,)), ((), ())), preferred_element_type=F32)


def _rms(x, g):
    return x * lax.rsqrt(jnp.mean(x * x, axis=-1, keepdims=True) + EPS) * g


def _silu(x):
    return x * jax.nn.sigmoid(x)


def _softplus(x):
    return jnp.maximum(x, 0.0) + jnp.log1p(jnp.exp(-jnp.abs(x)))


def _const_spec(shape):
    nd = len(shape)
    return pl.BlockSpec(shape, lambda *_: (0,) * nd, pipeline_mode=pl.Buffered(1))


def _swiglu(h_bf, wg_ref, wu_ref, wd_ref, act_ref):
    d_ff = wg_ref.shape[1]
    for c in range(0, d_ff, FF_CHUNK):
        gate = _dot(h_bf, wg_ref[:, c:c + FF_CHUNK])
        up = _dot(h_bf, wu_ref[:, c:c + FF_CHUNK])
        act_ref[:, c:c + FF_CHUNK] = (_silu(gate) * up).astype(BF16)
    return _dot(act_ref[...], wd_ref[...])


def _ffn_in_kernel(x_ref, g1_ref, wg_ref, wu_ref, wd_ref, gm_ref,
                   wqkv_ref, wz_ref, wglu_ref, wq_ref, wkv_ref, wab_ref,
                   x1_ref, qkv_ref, z_ref, glu_ref, q_ref, kv_ref, ab_ref,
                   act_ref):
    x = x_ref[...]
    h = _rms(x, g1_ref[...]).astype(BF16)
    x1 = x + 0.5 * _swiglu(h, wg_ref, wu_ref, wd_ref, act_ref)
    x1_ref[...] = x1
    h2 = _rms(x1, gm_ref[...]).astype(BF16)
    qkv_ref[...] = _dot(h2, wqkv_ref[...]).astype(BF16)
    z_ref[...] = _dot(h2, wz_ref[...]).astype(BF16)
    glu_ref[...] = _dot(h2, wglu_ref[...]).astype(BF16)
    q_ref[...] = _dot(h2, wq_ref[...]).astype(BF16)
    kv_ref[...] = _dot(h2, wkv_ref[...]).astype(BF16)
    ab_ref[...] = _dot(h2, wab_ref[...])


def _ffn_in_call(x, p):
    rows, d = x.shape
    tm = min(ROW_TILE, rows)
    d_ff = p['wg1'].shape[1]
    weights = [p['g1'], p['wg1'], p['wu1'], p['wd1'], p['gm'],
               p['w_qkv'], p['w_z'], p['w_glu'], p['w_q'], p['w_kv'], p['w_ab']]
    out_cols = [(d, F32)] + [(w.shape[1], BF16) for w in weights[5:10]] + [(LANES, F32)]
    row_spec = lambda n: pl.BlockSpec((tm, n), lambda i: (i, 0))
    return pl.pallas_call(
        _ffn_in_kernel,
        grid=(rows // tm,),
        in_specs=[row_spec(d)] + [_const_spec(w.shape) for w in weights],
        out_specs=[row_spec(n) for n, _ in out_cols],
        out_shape=[jax.ShapeDtypeStruct((rows, n), dt) for n, dt in out_cols],
        scratch_shapes=[pltpu.VMEM((tm, d_ff), BF16)],
        compiler_params=pltpu.CompilerParams(
            dimension_semantics=("parallel",), vmem_limit_bytes=VMEM_LIMIT),
        name="ffn_in",
    )(x, *weights)


def _out_ffn_kernel(x1_ref, of_ref, ob_ref, z_ref, ocv_ref, oat_ref, og_ref, wo_ref,
                    g2_ref, wg_ref, wu_ref, wd_ref, gf_ref, y_ref, mix_ref, act_ref):
    dn_cols = of_ref.shape[1]
    for h in range(dn_cols // DN_DV):
        sl = slice(h * DN_DV, (h + 1) * DN_DV)
        o = _rms(of_ref[:, sl] + ob_ref[:, sl], og_ref[...])
        mix_ref[:, sl] = (o * _silu(z_ref[:, sl].astype(F32))).astype(BF16)
    cv_cols = ocv_ref.shape[1]
    mix_ref[:, dn_cols:dn_cols + cv_cols] = ocv_ref[...]
    mix_ref[:, dn_cols + cv_cols:] = oat_ref[...]
    x2 = x1_ref[...] + _dot(mix_ref[...], wo_ref[...])
    h2 = _rms(x2, g2_ref[...]).astype(BF16)
    x3 = x2 + 0.5 * _swiglu(h2, wg_ref, wu_ref, wd_ref, act_ref)
    y_ref[...] = _rms(x3, gf_ref[...])


def _out_ffn_call(x1, o_f, o_b, z, o_cv, o_at, p):
    rows, d = x1.shape
    tm = min(ROW_TILE, rows)
    d_ff = p['wg2'].shape[1]
    weights = [p['og'], p['w_out'], p['g2'], p['wg2'], p['wu2'], p['wd2'], p['gf']]
    acts = [x1, o_f, o_b, z, o_cv, o_at]
    row_spec = lambda n: pl.BlockSpec((tm, n), lambda i: (i, 0))
    return pl.pallas_call(
        _out_ffn_kernel,
        grid=(rows // tm,),
        in_specs=[row_spec(a.shape[1]) for a in acts] + [_const_spec(w.shape) for w in weights],
        out_specs=row_spec(d),
        out_shape=jax.ShapeDtypeStruct((rows, d), F32),
        scratch_shapes=[pltpu.VMEM((tm, d), BF16), pltpu.VMEM((tm, d_ff), BF16)],
        compiler_params=pltpu.CompilerParams(
            dimension_semantics=("parallel",), vmem_limit_bytes=VMEM_LIMIT),
        name="out_ffn",
    )(*acts, *weights)


def _halo_specs(tt, halo, t_len, cols):
    per = tt // halo
    last = t_len // halo - 1
    prev = pl.BlockSpec((None, halo, cols), lambda b, i: (b, jnp.maximum(i * per - 1, 0), 0))
    cur = pl.BlockSpec((None, tt, cols), lambda b, i: (b, i, 0))
    nxt = pl.BlockSpec((None, halo, cols), lambda b, i: (b, jnp.minimum((i + 1) * per, last), 0))
    return [prev, cur, nxt]


def _fill_with_halo(dst_ref, prev, cur, nxt, halo, tt):
    i = pl.program_id(1)
    last = pl.num_programs(1) - 1
    dst_ref[0:halo, :] = jnp.where(i > 0, prev, jnp.zeros_like(prev))
    dst_ref[halo:halo + tt, :] = cur
    dst_ref[halo + tt:halo + tt + halo, :] = jnp.where(i < last, nxt, jnp.zeros_like(nxt))


def _dn_prep_kernel(prev_ref, cur_ref, next_ref, w_ref, out_ref, xw_ref, *, tt, halo):
    _fill_with_halo(xw_ref, prev_ref[...], cur_ref[...], next_ref[...], halo, tt)
    cols = cur_ref.shape[1]
    left = (DN_CONV - 1) // 2
    qk_groups = 2 * DN_HEADS * DN_DK // LANES
    q_groups = DN_HEADS * DN_DK // LANES
    win_rows = CONV_ROWS + 2 * halo
    out_row = lax.broadcasted_iota(jnp.int32, (DN_CONV * CONV_ROWS, win_rows), 0)
    src_row = lax.broadcasted_iota(jnp.int32, (DN_CONV * CONV_ROWS, win_rows), 1)
    tap = out_row // CONV_ROWS
    shift = (src_row == (out_row - tap * CONV_ROWS) + (halo - left) + tap).astype(BF16)
    group = 2 * LANES
    for rb in range(tt // CONV_ROWS):
        r0 = rb * CONV_ROWS
        for j in range(cols // group):
            gs = slice(j * group, (j + 1) * group)
            taps = _dot(shift, xw_ref[r0:r0 + win_rows, gs])
            acc = None
            for k in range(DN_CONV):
                term = taps[k * CONV_ROWS:(k + 1) * CONV_ROWS, :] * w_ref[k:k + 1, gs]
                acc = term if acc is None else acc + term
            y = _silu(acc)
            for e in range(group // LANES):
                jj = j * (group // LANES) + e
                ye = y[:, e * LANES:(e + 1) * LANES]
                if jj < qk_groups:
                    ye = ye * lax.rsqrt(jnp.sum(ye * ye, axis=-1, keepdims=True) + EPS)
                if jj < q_groups:
                    ye = ye * (DN_DK ** -0.5)
                out_ref[r0:r0 + CONV_ROWS, jj * LANES:(jj + 1) * LANES] = ye.astype(BF16)


def _dn_prep_call(qkv, conv_w):
    b, t_len, cols = qkv.shape
    tt = min(TIME_TILE, t_len)
    halo = BF16_SUBLANES
    return pl.pallas_call(
        functools.partial(_dn_prep_kernel, tt=tt, halo=halo),
        grid=(b, t_len // tt),
        in_specs=_halo_specs(tt, halo, t_len, cols) + [pl.BlockSpec(conv_w.shape, lambda b, i: (0, 0))],
        out_specs=pl.BlockSpec((None, tt, cols), lambda b, i: (b, i, 0)),
        out_shape=jax.ShapeDtypeStruct((b, t_len, cols), BF16),
        scratch_shapes=[pltpu.VMEM((tt + 2 * halo, cols), BF16)],
        compiler_params=pltpu.CompilerParams(
            dimension_semantics=("parallel", "parallel"), vmem_limit_bytes=VMEM_LIMIT),
        name="dn_prep",
    )(qkv, qkv, qkv, conv_w)


def _conf_kernel(prev_ref, cur_ref, next_ref, dw_ref, b_ref, g_ref, bb_ref, out_ref, sh_ref,
                 *, tt, halo):
    ch = out_ref.shape[1]

    def glu(x_ref):
        x = x_ref[...].astype(F32)
        return x[:, :ch] * jax.nn.sigmoid(x[:, ch:])

    _fill_with_halo(sh_ref.at[0], glu(prev_ref), glu(cur_ref), glu(next_ref), halo, tt)
    left = (CV_KERNEL - 1) // 2
    span = CONV_ROWS + 2 * halo - F32_SUBLANES
    for rb in range(tt // CONV_ROWS):
        r0 = rb * CONV_ROWS
        win = sh_ref[0, r0:r0 + CONV_ROWS + 2 * halo, :]
        for r in range(1, F32_SUBLANES):
            sh_ref[r, r0:r0 + span, :] = win[r:r + span, :]
        acc = None
        for k in range(CV_KERNEL):
            off = halo - left + k
            res = off % F32_SUBLANES
            base = r0 + off - res
            term = sh_ref[res, base:base + CONV_ROWS, :] * dw_ref[k:k + 1, :]
            acc = term if acc is None else acc + term
        acc = acc + b_ref[...]
        xc = acc - jnp.mean(acc, axis=-1, keepdims=True)
        var = jnp.mean(xc * xc, axis=-1, keepdims=True)
        y = xc * lax.rsqrt(var + EPS) * g_ref[...] + bb_ref[...]
        out_ref[r0:r0 + CONV_ROWS, :] = _silu(y).astype(BF16)


def _conf_call(glu_in, dw, dw_b, ln_g, ln_b):
    b, t_len, cols = glu_in.shape
    ch = cols // 2
    tt = min(TIME_TILE, t_len)
    halo = BF16_SUBLANES
    small = [dw, dw_b, ln_g, ln_b]
    return pl.pallas_call(
        functools.partial(_conf_kernel, tt=tt, halo=halo),
        grid=(b, t_len // tt),
        in_specs=_halo_specs(tt, halo, t_len, cols)
        + [pl.BlockSpec(a.shape, lambda b, i: (0, 0)) for a in small],
        out_specs=pl.BlockSpec((None, tt, ch), lambda b, i: (b, i, 0)),
        out_shape=jax.ShapeDtypeStruct((b, t_len, ch), BF16),
        scratch_shapes=[pltpu.VMEM((F32_SUBLANES, tt + 2 * halo, ch), F32)],
        compiler_params=pltpu.CompilerParams(
            dimension_semantics=("parallel", "parallel"), vmem_limit_bytes=VMEM_LIMIT),
        name="conformer",
    )(glu_in, glu_in, glu_in, *small)


def _split_dot(m_bf, x):
    hi = x.astype(BF16)
    r1 = x - hi.astype(F32)
    mid = r1.astype(BF16)
    lo = (r1 - mid.astype(F32)).astype(BF16)
    return _dot(m_bf, hi) + _dot(m_bf, mid) + _dot(m_bf, lo)


def _dn_scan_kernel(qf_ref, qb_ref, abf_ref, abb_ref, pv_ref, of_ref, ob_ref, s_ref, *, tt, c):
    @pl.when(pl.program_id(1) == 0)
    def _():
        s_ref[...] = jnp.zeros_like(s_ref)

    nchunk = tt // c
    row = lax.broadcasted_iota(jnp.int32, (c, c), 0)
    col = lax.broadcasted_iota(jnp.int32, (c, c), 1)
    incl = (row >= col, row <= col)
    strict = (row > col, row < col)
    tri_bf = (incl[0].astype(BF16), incl[1].astype(BF16))
    dirs = ((qf_ref, abf_ref, of_ref), (qb_ref, abb_ref, ob_ref))
    hk = DN_HEADS * DN_DK
    gate_lane = 2 * DN_HEADS

    units = [(d, h) for d in range(2) for h in range(DN_HEADS)]

    levels = []
    shift = 0
    while (1 << shift) < c:
        bi = row >> shift
        bj = col >> shift
        levels.append((((bi & 1) == 1) & (bj == bi - 1), ((bj & 1) == 1) & (bi == bj - 1)))
        shift += 1

    def prepare(n):
        gates = []
        for d in range(2):
            nn = n if d == 0 else nchunk - 1 - n
            r0 = pl.multiple_of(nn * c, c)
            ab = dirs[d][1][pl.ds(r0, c), :]
            beta = jax.nn.sigmoid(ab)
            g = -jnp.exp(pv_ref[0:1, :]) * _softplus(ab + pv_ref[1:2, :])
            gc = _split_dot(tri_bf[d], g)
            tot = gc[c - 1:c, :] if d == 0 else gc[0:1, :]
            gates.append(dict(r0=r0, beta=beta, gc=gc, gc_t=gc.T, e_gc=jnp.exp(gc),
                              e_kd=jnp.exp(tot - gc), e_tot=jnp.exp(tot)))
        st = []
        for d, h in units:
            ga = gates[d]
            x_ref = dirs[d][0]
            j = d * DN_HEADS + h
            gj = gate_lane + j
            rows = pl.ds(ga['r0'], c)
            q = x_ref[rows, h * DN_DK:(h + 1) * DN_DK]
            k = x_ref[rows, hk + h * DN_DK:hk + (h + 1) * DN_DK]
            v = x_ref[rows, 2 * hk + h * DN_DV:2 * hk + (h + 1) * DN_DV]
            kf = k.astype(F32)
            b_col = ga['beta'][:, j:j + 1]
            kb = kf * b_col
            eg_col = ga['e_gc'][:, gj:gj + 1]
            diff = ga['gc'][:, gj:gj + 1] - ga['gc_t'][gj:gj + 1, :]
            decay = jnp.exp(jnp.where(incl[d], diff, -jnp.inf))
            gq = _dot_nt(jnp.concatenate([kb.astype(BF16), q], axis=0), k)
            st.append(dict(
                d=d, h=h, j=j, rows=rows,
                a=jnp.where(strict[d], gq[:c] * decay, 0.0),
                qk=(gq[c:] * decay).astype(BF16),
                xs=jnp.concatenate([v.astype(F32) * b_col, kb * eg_col], axis=1),
                qd=(q.astype(F32) * eg_col).astype(BF16),
                kd=(kf * ga['e_kd'][:, gj:gj + 1]).astype(BF16),
                e_tot=ga['e_tot'][:, gj:gj + 1]))
        return st

    def solve(st):
        for u in st:
            u['r'] = -jnp.where(levels[0][u['d']], u['a'], 0.0)
        for masks in levels[1:]:
            for u in st:
                e = jnp.where(masks[u['d']], u['a'], 0.0)
                u['x'] = e + _dot(u['r'].astype(BF16), e.astype(BF16))
            for u in st:
                u['r'] = u['r'] - u['x'] - _dot(u['x'].astype(BF16), u['r'].astype(BF16))
        for u in st:
            u['uw'] = u['xs'] + _dot(u['r'].astype(BF16), u['xs'].astype(BF16))

    def advance(st):
        for u in st:
            u['s'] = s_ref[u['j']]
            w_bf = u['uw'][:, DN_DV:].astype(BF16)
            u['wq'] = _dot(jnp.concatenate([w_bf, u['qd']], axis=0), u['s'].astype(BF16))
        for u in st:
            u['vn'] = (u['uw'][:, :DN_DV] - u['wq'][:c]).astype(BF16)
            s_ref[u['j']] = u['s'] * u['e_tot'] + _dot_tn(u['kd'], u['vn'])
        for u in st:
            o = u['wq'][c:] + _dot(u['qk'], u['vn'])
            dirs[u['d']][2][u['rows'], u['h'] * DN_DV:(u['h'] + 1) * DN_DV] = o

    group = 2 if nchunk % 2 == 0 else 1

    def chunk_body(n, carry):
        sts = [prepare(n * group + e) for e in range(group)]
        solve([u for st in sts for u in st])
        for st in sts:
            advance(st)
        return carry

    lax.fori_loop(0, nchunk // group, chunk_body, 0, unroll=True)


def _dn_scan_call(qkvn, ab, pvec):
    b, t_len, cols = qkvn.shape
    tt = min(TIME_TILE, t_len)
    nt = t_len // tt
    c = min(DN_CHUNK, tt)
    dv = DN_HEADS * DN_DV
    fwd = lambda n: pl.BlockSpec((None, tt, n), lambda b, i: (b, i, 0))
    bwd = lambda n: pl.BlockSpec((None, tt, n), lambda b, i: (b, nt - 1 - i, 0))
    return pl.pallas_call(
        functools.partial(_dn_scan_kernel, tt=tt, c=c),
        grid=(b, nt),
        in_specs=[fwd(cols), bwd(cols), fwd(LANES), bwd(LANES),
                  pl.BlockSpec(pvec.shape, lambda b, i: (0, 0))],
        out_specs=[fwd(dv), bwd(dv)],
        out_shape=[jax.ShapeDtypeStruct((b, t_len, dv), F32)] * 2,
        scratch_shapes=[pltpu.VMEM((2 * DN_HEADS, DN_DK, DN_DV), F32)],
        compiler_params=pltpu.CompilerParams(
            dimension_semantics=("parallel", "arbitrary"), vmem_limit_bytes=VMEM_LIMIT),
        name="dn_scan",
    )(qkvn, qkvn, ab, ab, pvec)


def _attn_kernel(q_ref, kp_ref, kc_ref, kn_ref, qg_ref, kg_ref, sink_ref, out_ref,
                 kd_ref, vd_ref, bias_ref, *, tq, w):
    i = pl.program_id(1)
    nblk = pl.num_programs(1) * (tq // w)
    d = AT_HEAD_DIM
    lane = lax.broadcasted_iota(jnp.int32, (1, 2 * d), 1)
    lo = lane < d

    @pl.when(i == 0)
    def _():
        qi = lax.broadcasted_iota(jnp.int32, (w, 3 * w), 0)
        kj = lax.broadcasted_iota(jnp.int32, (w, 3 * w), 1)
        dist = jnp.abs(qi - (kj - w))
        for variant in range(4):
            ok = dist <= w
            if variant & 1:
                ok = ok & (kj >= w)
            if variant & 2:
                ok = ok & (kj < 2 * w)
            for h in range(AT_HEADS):
                slope = 2.0 ** (-ALIBI_MAX * (h + 1) / AT_HEADS)
                bias_ref[variant * AT_HEADS + h] = jnp.where(ok, (-slope * LOG2E) * dist.astype(F32), -jnp.inf)

    def head_rms(x, g):
        sq = x * x
        s_lo = jnp.sum(jnp.where(lo, sq, 0.0), axis=-1, keepdims=True)
        s_hi = jnp.sum(jnp.where(lo, 0.0, sq), axis=-1, keepdims=True)
        ms = jnp.where(lo, s_lo, s_hi) * (1.0 / d)
        return x * lax.rsqrt(ms + EPS) * g

    def fill_kv(src_ref, r0, rows):
        x = src_ref[...]
        kn = head_rms(x[:, :2 * d].astype(F32), kg_ref[...])
        vf = x[:, 2 * d:].astype(F32)
        k_roll = pltpu.roll(kn, d, axis=1)
        v_roll = pltpu.roll(vf, d, axis=1)
        kd_ref[0, r0:r0 + rows, :] = jnp.where(lo, kn, k_roll).astype(BF16)
        kd_ref[1, r0:r0 + rows, :] = jnp.where(lo, k_roll, kn).astype(BF16)
        vd_ref[0, r0:r0 + rows, :] = jnp.where(lo, vf, v_roll).astype(BF16)
        vd_ref[1, r0:r0 + rows, :] = jnp.where(lo, v_roll, vf).astype(BF16)

    fill_kv(kp_ref, 0, w)
    fill_kv(kc_ref, w, tq)
    fill_kv(kn_ref, w + tq, w)

    group = AT_HEADS // AT_KV_HEADS
    nsub = tq // w
    units = [(jb, g) for jb in range(nsub) for g in range(AT_KV_HEADS)]

    def scores(jb, g):
        qn = head_rms(q_ref[jb * w:(jb + 1) * w, g * 2 * d:(g + 1) * 2 * d].astype(F32),
                      qg_ref[...]) * (d ** -0.5 * LOG2E)
        lhs = jnp.concatenate([jnp.where(lo, qn, 0.0), jnp.where(lo, 0.0, qn)], axis=0)
        return _dot_nt(lhs.astype(BF16), kd_ref[g, jb * w:jb * w + 3 * w, :])

    ready = {}
    for idx, (jb, g) in enumerate(units):
        if idx % ATTN_AHEAD == 0:
            for nxt in range(idx, min(idx + ATTN_AHEAD, len(units))):
                ready[nxt] = scores(*units[nxt])
        s = ready.pop(idx)
        n = i * nsub + jb
        variant = None
        if jb == 0:
            variant = (n == 0).astype(jnp.int32)
        if jb == nsub - 1:
            at_end = 2 * (n == nblk - 1).astype(jnp.int32)
            variant = at_end if variant is None else variant + at_end
        ps = []
        inv = []
        for hh in range(group):
            h = g * group + hh
            sink = sink_ref[h] * LOG2E
            bias = bias_ref[h] if variant is None else bias_ref[variant * AT_HEADS + h]
            sh = s[hh * w:(hh + 1) * w] + bias
            m = jnp.maximum(jnp.max(sh, axis=-1, keepdims=True), sink)
            p = jnp.exp2(sh - m)
            den = jnp.sum(p, axis=-1, keepdims=True) + jnp.exp2(sink - m)
            ps.append(p.astype(BF16))
            inv.append(1.0 / den)
        pv = _dot(jnp.concatenate(ps, axis=0), vd_ref[g, jb * w:jb * w + 3 * w, :])
        o = jnp.where(lo, pv[:w] * inv[0], pv[w:] * inv[1])
        out_ref[jb * w:(jb + 1) * w, g * 2 * d:(g + 1) * 2 * d] = o.astype(BF16)


def _attn_call(q, kv, q_g2, k_g2, sink):
    b, t_len, qc = q.shape
    w = AT_WINDOW
    tq = min(ATTN_TILE, t_len)
    kvc = kv.shape[2]
    two_d = 2 * AT_HEAD_DIM
    return pl.pallas_call(
        functools.partial(_attn_kernel, tq=tq, w=w),
        grid=(b, t_len // tq),
        in_specs=[pl.BlockSpec((None, tq, qc), lambda b, i: (b, i, 0))]
        + _halo_specs(tq, w, t_len, kvc)
        + [pl.BlockSpec(q_g2.shape, lambda b, i: (0, 0)),
           pl.BlockSpec(k_g2.shape, lambda b, i: (0, 0)),
           pl.BlockSpec(memory_space=pltpu.SMEM)],
        out_specs=pl.BlockSpec((None, tq, qc), lambda b, i: (b, i, 0)),
        out_shape=jax.ShapeDtypeStruct((b, t_len, qc), BF16),
        scratch_shapes=[pltpu.VMEM((AT_KV_HEADS, tq + 2 * w, two_d), BF16),
                        pltpu.VMEM((AT_KV_HEADS, tq + 2 * w, two_d), BF16),
                        pltpu.VMEM((4 * AT_HEADS, w, 3 * w), F32)],
        compiler_params=pltpu.CompilerParams(
            dimension_semantics=("parallel", "arbitrary"), vmem_limit_bytes=VMEM_LIMIT),
        name="band_attn",
    )(q, kv, kv, kv, q_g2, k_g2, sink)


def _layer_params(l, ffn1_norm, ffn1_w_gate, ffn1_w_up, ffn1_w_down, mix_norm, w_in,
                  dn_conv, dn_a_log, dn_dt_bias, dn_out_norm, cv_dw, cv_dw_bias, cv_ln_g, cv_ln_b,
                  at_q_norm, at_k_norm, at_sink, w_out, ffn2_norm, ffn2_w_gate, ffn2_w_up,
                  ffn2_w_down, final_norm):
    row = lambda a: a[l].reshape(1, -1).astype(F32)
    bf = lambda a: a.astype(BF16)
    hk = DN_HEADS * DN_DK
    hv = DN_HEADS * DN_DV
    cv_ch = cv_dw.shape[2]
    sizes = (2 * hk + hv, hv, 2 * DN_HEADS, 2 * DN_HEADS, 2 * cv_ch,
             AT_HEADS * AT_HEAD_DIM, 2 * AT_KV_HEADS * AT_HEAD_DIM)
    offs = [0]
    for s in sizes:
        offs.append(offs[-1] + s)
    wi = w_in[l]
    piece = lambda n: wi[:, offs[n]:offs[n + 1]]
    n_gate = 4 * DN_HEADS
    w_ab = jnp.pad(jnp.concatenate([piece(2), piece(3)], axis=1), ((0, 0), (0, LANES - n_gate)))
    pad_gate = lambda a: jnp.pad(a.reshape(1, -1).astype(F32),
                                 ((0, 0), (2 * DN_HEADS, LANES - n_gate)))
    pvec = jnp.concatenate([pad_gate(dn_a_log[l]), pad_gate(dn_dt_bias[l])], axis=0)
    two = lambda a: jnp.concatenate([row(a), row(a)], axis=1)
    return dict(
        g1=row(ffn1_norm), wg1=bf(ffn1_w_gate[l]), wu1=bf(ffn1_w_up[l]), wd1=bf(ffn1_w_down[l]),
        gm=row(mix_norm), w_qkv=bf(piece(0)), w_z=bf(piece(1)), w_ab=bf(w_ab), w_glu=bf(piece(4)),
        w_q=bf(piece(5)), w_kv=bf(piece(6)),
        dn_conv=dn_conv[l].astype(F32), pvec=pvec, og=row(dn_out_norm),
        cv_dw=cv_dw[l].astype(F32), cv_b=row(cv_dw_bias), cv_g=row(cv_ln_g), cv_bb=row(cv_ln_b),
        q_g2=two(at_q_norm), k_g2=two(at_k_norm), sink=at_sink[l].astype(F32),
        w_out=bf(w_out[l]), g2=row(ffn2_norm), wg2=bf(ffn2_w_gate[l]), wu2=bf(ffn2_w_up[l]),
        wd2=bf(ffn2_w_down[l]), gf=row(final_norm))


def _layer(x, b, t_len, p):
    x1, qkv, z, glu, q_at, kv_at, ab = _ffn_in_call(x, p)
    seq = lambda a: a.reshape(b, t_len, a.shape[-1])
    flat = lambda a: a.reshape(b * t_len, a.shape[-1])
    qkvn = _dn_prep_call(seq(qkv), p['dn_conv'])
    o_f, o_b = _dn_scan_call(qkvn, seq(ab), p['pvec'])
    o_cv = _conf_call(seq(glu), p['cv_dw'], p['cv_b'], p['cv_g'], p['cv_bb'])
    o_at = _attn_call(seq(q_at), seq(kv_at), p['q_g2'], p['k_g2'], p['sink'])
    return _out_ffn_call(x1, flat(o_f), flat(o_b), z, flat(o_cv), flat(o_at), p)


def kernel(x_prompt, x_sample, ffn1_norm, ffn1_w_gate, ffn1_w_up, ffn1_w_down, mix_norm, w_in, dn_conv, dn_a_log, dn_dt_bias, dn_out_norm, cv_dw, cv_dw_bias, cv_ln_g, cv_ln_b, at_q_norm, at_k_norm, at_sink, w_out, ffn2_norm, ffn2_w_gate, ffn2_w_up, ffn2_w_down, final_norm):
    params = (ffn1_norm, ffn1_w_gate, ffn1_w_up, ffn1_w_down, mix_norm, w_in,
              dn_conv, dn_a_log, dn_dt_bias, dn_out_norm, cv_dw, cv_dw_bias, cv_ln_g, cv_ln_b,
              at_q_norm, at_k_norm, at_sink, w_out, ffn2_norm, ffn2_w_gate, ffn2_w_up, ffn2_w_down,
              final_norm)
    depth = ffn1_norm.shape[0]
    layers = [_layer_params(l, *params) for l in range(depth)]

    def trunk(x):
        b, t_len, d = x.shape
        y = x.reshape(b * t_len, d)
        for p in layers:
            y = _layer(y, b, t_len, p)
        return y.reshape(b, t_len, d)

    return (trunk(x_prompt), trunk(x_sample))
```

```python
import functools

import jax
import jax.numpy as jnp
from jax import lax
from jax.experimental import pallas as pl
from jax.experimental.pallas import tpu as pltpu

F32 = jnp.float32
BF16 = jnp.bfloat16
EPS = 1e-6

DN_HEADS = 4
DN_DK = 128
DN_DV = 128
DN_CONV = 5
CV_KERNEL = 31
AT_HEADS = 4
AT_KV_HEADS = 2
AT_HEAD_DIM = 64
AT_WINDOW = 128
ALIBI_MAX = 8.0
LOG2E = 1.4426950408889634

LANES = 128
BF16_SUBLANES = 16
F32_SUBLANES = 8

ROW_TILE = 512
FF_CHUNK = 256
TIME_TILE = 512
ATTN_TILE = 4096
ATTN_AHEAD = 16
DN_CHUNK = 128
CONV_ROWS = 64
VMEM_LIMIT = 56 * 1024 * 1024


def _dot(a, b):
    return jnp.dot(a, b, preferred_element_type=F32)


def _dot_nt(a, b):
    return lax.dot_general(a, b, (((1,), (1,)), ((), ())), preferred_element_type=F32)


def _dot_tn(a, b):
    return lax.dot_general(a, b, (((0,), (0,)), ((), ())), preferred_element_type=F32)


def _rms(x, g):
    return x * lax.rsqrt(jnp.mean(x * x, axis=-1, keepdims=True) + EPS) * g


def _silu(x):
    return x * jax.nn.sigmoid(x)


def _softplus(x):
    return jnp.maximum(x, 0.0) + jnp.log1p(jnp.exp(-jnp.abs(x)))


def _const_spec(shape):
    nd = len(shape)
    return pl.BlockSpec(shape, lambda *_: (0,) * nd, pipeline_mode=pl.Buffered(1))


def _swiglu(h_bf, wg_ref, wu_ref, wd_ref, act_ref):
    d_ff = wg_ref.shape[1]
    for c in range(0, d_ff, FF_CHUNK):
        gate = _dot(h_bf, wg_ref[:, c:c + FF_CHUNK])
        up = _dot(h_bf, wu_ref[:, c:c + FF_CHUNK])
        act_ref[:, c:c + FF_CHUNK] = (_silu(gate) * up).astype(BF16)
    return _dot(act_ref[...], wd_ref[...])


def _ffn_in_kernel(x_ref, g1_ref, wg_ref, wu_ref, wd_ref, gm_ref,
                   wqkv_ref, wz_ref, wglu_ref, wq_ref, wkv_ref, wab_ref,
                   x1_ref, qkv_ref, z_ref, glu_ref, q_ref, kv_ref, ab_ref,
                   act_ref):
    x = x_ref[...]
    h = _rms(x, g1_ref[...]).astype(BF16)
    x1 = x + 0.5 * _swiglu(h, wg_ref, wu_ref, wd_ref, act_ref)
    x1_ref[...] = x1
    h2 = _rms(x1, gm_ref[...]).astype(BF16)
    qkv_ref[...] = _dot(h2, wqkv_ref[...]).astype(BF16)
    z_ref[...] = _dot(h2, wz_ref[...]).astype(BF16)
    glu_ref[...] = _dot(h2, wglu_ref[...]).astype(BF16)
    q_ref[...] = _dot(h2, wq_ref[...]).astype(BF16)
    kv_ref[...] = _dot(h2, wkv_ref[...]).astype(BF16)
    ab_ref[...] = _dot(h2, wab_ref[...])


def _ffn_in_call(x, p):
    rows, d = x.shape
    tm = min(ROW_TILE, rows)
    d_ff = p['wg1'].shape[1]
    weights = [p['g1'], p['wg1'], p['wu1'], p['wd1'], p['gm'],
               p['w_qkv'], p['w_z'], p['w_glu'], p['w_q'], p['w_kv'], p['w_ab']]
    out_cols = [(d, F32)] + [(w.shape[1], BF16) for w in weights[5:10]] + [(LANES, F32)]
    row_spec = lambda n: pl.BlockSpec((tm, n), lambda i: (i, 0))
    return pl.pallas_call(
        _ffn_in_kernel,
        grid=(rows // tm,),
        in_specs=[row_spec(d)] + [_const_spec(w.shape) for w in weights],
        out_specs=[row_spec(n) for n, _ in out_cols],
        out_shape=[jax.ShapeDtypeStruct((rows, n), dt) for n, dt in out_cols],
        scratch_shapes=[pltpu.VMEM((tm, d_ff), BF16)],
        compiler_params=pltpu.CompilerParams(
            dimension_semantics=("parallel",), vmem_limit_bytes=VMEM_LIMIT),
        name="ffn_in",
    )(x, *weights)


def _out_ffn_kernel(x1_ref, of_ref, ob_ref, z_ref, ocv_ref, oat_ref, og_ref, wo_ref,
                    g2_ref, wg_ref, wu_ref, wd_ref, gf_ref, y_ref, mix_ref, act_ref):
    dn_cols = of_ref.shape[1]
    for h in range(dn_cols // DN_DV):
        sl = slice(h * DN_DV, (h + 1) * DN_DV)
        o = _rms(of_ref[:, sl] + ob_ref[:, sl], og_ref[...])
        mix_ref[:, sl] = (o * _silu(z_ref[:, sl].astype(F32))).astype(BF16)
    cv_cols = ocv_ref.shape[1]
    mix_ref[:, dn_cols:dn_cols + cv_cols] = ocv_ref[...]
    mix_ref[:, dn_cols + cv_cols:] = oat_ref[...]
    x2 = x1_ref[...] + _dot(mix_ref[...], wo_ref[...])
    h2 = _rms(x2, g2_ref[...]).astype(BF16)
    x3 = x2 + 0.5 * _swiglu(h2, wg_ref, wu_ref, wd_ref, act_ref)
    y_ref[...] = _rms(x3, gf_ref[...])


def _out_ffn_call(x1, o_f, o_b, z, o_cv, o_at, p):
    rows, d = x1.shape
    tm = min(ROW_TILE, rows)
    d_ff = p['wg2'].shape[1]
    weights = [p['og'], p['w_out'], p['g2'], p['wg2'], p['wu2'], p['wd2'], p['gf']]
    acts = [x1, o_f, o_b, z, o_cv, o_at]
    row_spec = lambda n: pl.BlockSpec((tm, n), lambda i: (i, 0))
    return pl.pallas_call(
        _out_ffn_kernel,
        grid=(rows // tm,),
        in_specs=[row_spec(a.shape[1]) for a in acts] + [_const_spec(w.shape) for w in weights],
        out_specs=row_spec(d),
        out_shape=jax.ShapeDtypeStruct((rows, d), F32),
        scratch_shapes=[pltpu.VMEM((tm, d), BF16), pltpu.VMEM((tm, d_ff), BF16)],
        compiler_params=pltpu.CompilerParams(
            dimension_semantics=("parallel",), vmem_limit_bytes=VMEM_LIMIT),
        name="out_ffn",
    )(*acts, *weights)


def _halo_specs(tt, halo, t_len, cols):
    per = tt // halo
    last = t_len // halo - 1
    prev = pl.BlockSpec((None, halo, cols), lambda b, i: (b, jnp.maximum(i * per - 1, 0), 0))
    cur = pl.BlockSpec((None, tt, cols), lambda b, i: (b, i, 0))
    nxt = pl.BlockSpec((None, halo, cols), lambda b, i: (b, jnp.minimum((i + 1) * per, last), 0))
    return [prev, cur, nxt]


def _fill_with_halo(dst_ref, prev, cur, nxt, halo, tt):
    i = pl.program_id(1)
    last = pl.num_programs(1) - 1
    dst_ref[0:halo, :] = jnp.where(i > 0, prev, jnp.zeros_like(prev))
    dst_ref[halo:halo + tt, :] = cur
    dst_ref[halo + tt:halo + tt + halo, :] = jnp.where(i < last, nxt, jnp.zeros_like(nxt))


def _dn_prep_rows(xw_ref, w_ref, out_ref, shift, r0, halo):
    cols = out_ref.shape[1]
    qk_groups = 2 * DN_HEADS * DN_DK // LANES
    q_groups = DN_HEADS * DN_DK // LANES
    win_rows = CONV_ROWS + 2 * halo
    group = 2 * LANES
    for j in range(cols // group):
        gs = slice(j * group, (j + 1) * group)
        taps = _dot(shift, xw_ref[r0:r0 + win_rows, gs])
        acc = None
        for k in range(DN_CONV):
            term = taps[k * CONV_ROWS:(k + 1) * CONV_ROWS, :] * w_ref[k:k + 1, gs]
            acc = term if acc is None else acc + term
        y = _silu(acc)
        for e in range(group // LANES):
            jj = j * (group // LANES) + e
            ye = y[:, e * LANES:(e + 1) * LANES]
            if jj < qk_groups:
                ye = ye * lax.rsqrt(jnp.sum(ye * ye, axis=-1, keepdims=True) + EPS)
            if jj < q_groups:
                ye = ye * (DN_DK ** -0.5)
            out_ref[r0:r0 + CONV_ROWS, jj * LANES:(jj + 1) * LANES] = ye.astype(BF16)


def _conformer_rows(sh_ref, dw_ref, b_ref, g_ref, bb_ref, out_ref, r0, halo):
    left = (CV_KERNEL - 1) // 2
    span = CONV_ROWS + 2 * halo - F32_SUBLANES
    win = sh_ref[0, r0:r0 + CONV_ROWS + 2 * halo, :]
    for r in range(1, F32_SUBLANES):
        sh_ref[r, r0:r0 + span, :] = win[r:r + span, :]
    acc = None
    for k in range(CV_KERNEL):
        off = halo - left + k
        res = off % F32_SUBLANES
        base = r0 + off - res
        term = sh_ref[res, base:base + CONV_ROWS, :] * dw_ref[k:k + 1, :]
        acc = term if acc is None else acc + term
    acc = acc + b_ref[...]
    xc = acc - jnp.mean(acc, axis=-1, keepdims=True)
    var = jnp.mean(xc * xc, axis=-1, keepdims=True)
    y = xc * lax.rsqrt(var + EPS) * g_ref[...] + bb_ref[...]
    out_ref[r0:r0 + CONV_ROWS, :] = _silu(y).astype(BF16)


def _conv_mixers_kernel(qp_ref, qc_ref, qn_ref, gp_ref, gc_ref, gn_ref, cw_ref,
                        dw_ref, b_ref, g_ref, bb_ref, qkvn_ref, ocv_ref, xw_ref, sh_ref,
                        *, tt, halo):
    ch = ocv_ref.shape[1]

    def glu(x_ref):
        x = x_ref[...].astype(F32)
        return x[:, :ch] * jax.nn.sigmoid(x[:, ch:])

    _fill_with_halo(xw_ref, qp_ref[...], qc_ref[...], qn_ref[...], halo, tt)
    _fill_with_halo(sh_ref.at[0], glu(gp_ref), glu(gc_ref), glu(gn_ref), halo, tt)
    left = (DN_CONV - 1) // 2
    win_rows = CONV_ROWS + 2 * halo
    out_row = lax.broadcasted_iota(jnp.int32, (DN_CONV * CONV_ROWS, win_rows), 0)
    src_row = lax.broadcasted_iota(jnp.int32, (DN_CONV * CONV_ROWS, win_rows), 1)
    tap = out_row // CONV_ROWS
    shift = (src_row == (out_row - tap * CONV_ROWS) + (halo - left) + tap).astype(BF16)
    for r0 in range(0, tt, CONV_ROWS):
        _dn_prep_rows(xw_ref, cw_ref, qkvn_ref, shift, r0, halo)
        _conformer_rows(sh_ref, dw_ref, b_ref, g_ref, bb_ref, ocv_ref, r0, halo)


def _conv_mixers_call(qkv, glu_in, conv_w, dw, dw_b, ln_g, ln_b):
    b, t_len, cols = qkv.shape
    gcols = glu_in.shape[2]
    ch = gcols // 2
    tt = min(TIME_TILE, t_len)
    halo = BF16_SUBLANES
    small = [conv_w, dw, dw_b, ln_g, ln_b]
    tile = lambda n: pl.BlockSpec((None, tt, n), lambda b, i: (b, i, 0))
    return pl.pallas_call(
        functools.partial(_conv_mixers_kernel, tt=tt, halo=halo),
        grid=(b, t_len // tt),
        in_specs=_halo_specs(tt, halo, t_len, cols) + _halo_specs(tt, halo, t_len, gcols)
        + [pl.BlockSpec(a.shape, lambda b, i: (0, 0)) for a in small],
        out_specs=[tile(cols), tile(ch)],
        out_shape=[jax.ShapeDtypeStruct((b, t_len, cols), BF16),
                   jax.ShapeDtypeStruct((b, t_len, ch), BF16)],
        scratch_shapes=[pltpu.VMEM((tt + 2 * halo, cols), BF16),
                        pltpu.VMEM((F32_SUBLANES, tt + 2 * halo, ch), F32)],
        compiler_params=pltpu.CompilerParams(
            dimension_semantics=("parallel", "parallel"), vmem_limit_bytes=VMEM_LIMIT),
        name="conv_mixers",
    )(qkv, qkv, qkv, glu_in, glu_in, glu_in, *small)


def _split_dot(m_bf, x):
    hi = x.astype(BF16)
    r1 = x - hi.astype(F32)
    mid = r1.astype(BF16)
    lo = (r1 - mid.astype(F32)).astype(BF16)
    return _dot(m_bf, hi) + _dot(m_bf, mid) + _dot(m_bf, lo)


def _dn_scan_kernel(qf_ref, qb_ref, abf_ref, abb_ref, pv_ref, of_ref, ob_ref, s_ref, *, tt, c):
    @pl.when(pl.program_id(1) == 0)
    def _():
        s_ref[...] = jnp.zeros_like(s_ref)

    nchunk = tt // c
    row = lax.broadcasted_iota(jnp.int32, (c, c), 0)
    col = lax.broadcasted_iota(jnp.int32, (c, c), 1)
    incl = (row >= col, row <= col)
    strict = (row > col, row < col)
    tri_bf = (incl[0].astype(BF16), incl[1].astype(BF16))
    dirs = ((qf_ref, abf_ref, of_ref), (qb_ref, abb_ref, ob_ref))
    hk = DN_HEADS * DN_DK
    gate_lane = 2 * DN_HEADS

    units = [(d, h) for d in range(2) for h in range(DN_HEADS)]

    levels = []
    shift = 0
    while (1 << shift) < c:
        bi = row >> shift
        bj = col >> shift
        levels.append((((bi & 1) == 1) & (bj == bi - 1), ((bj & 1) == 1) & (bi == bj - 1)))
        shift += 1

    def prepare(n):
        gates = []
        for d in range(2):
            nn = n if d == 0 else nchunk - 1 - n
            r0 = pl.multiple_of(nn * c, c)
            ab = dirs[d][1][pl.ds(r0, c), :]
            beta = jax.nn.sigmoid(ab)
            g = -jnp.exp(pv_ref[0:1, :]) * _softplus(ab + pv_ref[1:2, :])
            gc = _split_dot(tri_bf[d], g)
            tot = gc[c - 1:c, :] if d == 0 else gc[0:1, :]
            gates.append(dict(r0=r0, beta=beta, gc=gc, gc_t=gc.T, e_gc=jnp.exp(gc),
                              e_kd=jnp.exp(tot - gc), e_tot=jnp.exp(tot)))
        st = []
        for d, h in units:
            ga = gates[d]
            x_ref = dirs[d][0]
            j = d * DN_HEADS + h
            gj = gate_lane + j
            rows = pl.ds(ga['r0'], c)
            q = x_ref[rows, h * DN_DK:(h + 1) * DN_DK]
            k = x_ref[rows, hk + h * DN_DK:hk + (h + 1) * DN_DK]
            v = x_ref[rows, 2 * hk + h * DN_DV:2 * hk + (h + 1) * DN_DV]
            kf = k.astype(F32)
            b_col = ga['beta'][:, j:j + 1]
            kb = kf * b_col
            eg_col = ga['e_gc'][:, gj:gj + 1]
            diff = ga['gc'][:, gj:gj + 1] - ga['gc_t'][gj:gj + 1, :]
            decay = jnp.exp(jnp.where(incl[d], diff, -jnp.inf))
            gq = _dot_nt(jnp.concatenate([kb.astype(BF16), q], axis=0), k)
            st.append(dict(
                d=d, h=h, j=j, rows=rows,
                a=jnp.where(strict[d], gq[:c] * decay, 0.0),
                qk=(gq[c:] * decay).astype(BF16),
                xs=jnp.concatenate([v.astype(F32) * b_col, kb * eg_col], axis=1),
                qd=(q.astype(F32) * eg_col).astype(BF16),
                kd=(kf * ga['e_kd'][:, gj:gj + 1]).astype(BF16),
                e_tot=ga['e_tot'][:, gj:gj + 1]))
        return st

    def solve(st):
        for u in st:
            u['r'] = -jnp.where(levels[0][u['d']], u['a'], 0.0)
        for masks in levels[1:]:
            for u in st:
                e = jnp.where(masks[u['d']], u['a'], 0.0)
                u['x'] = e + _dot(u['r'].astype(BF16), e.astype(BF16))
            for u in st:
                u['r'] = u['r'] - u['x'] - _dot(u['x'].astype(BF16), u['r'].astype(BF16))
        for u in st:
            u['uw'] = u['xs'] + _dot(u['r'].astype(BF16), u['xs'].astype(BF16))

    def advance(st):
        for u in st:
            u['s'] = s_ref[u['j']]
            w_bf = u['uw'][:, DN_DV:].astype(BF16)
            u['wq'] = _dot(jnp.concatenate([w_bf, u['qd']], axis=0), u['s'].astype(BF16))
        for u in st:
            u['vn'] = (u['uw'][:, :DN_DV] - u['wq'][:c]).astype(BF16)
            s_ref[u['j']] = u['s'] * u['e_tot'] + _dot_tn(u['kd'], u['vn'])
        for u in st:
            o = u['wq'][c:] + _dot(u['qk'], u['vn'])
            dirs[u['d']][2][u['rows'], u['h'] * DN_DV:(u['h'] + 1) * DN_DV] = o

    group = 2 if nchunk % 2 == 0 else 1

    def chunk_body(n, carry):
        sts = [prepare(n * group + e) for e in range(group)]
        solve([u for st in sts for u in st])
        for st in sts:
            advance(st)
        return carry

    lax.fori_loop(0, nchunk // group, chunk_body, 0, unroll=True)


def _dn_scan_call(qkvn, ab, pvec):
    b, t_len, cols = qkvn.shape
    tt = min(TIME_TILE, t_len)
    nt = t_len // tt
    c = min(DN_CHUNK, tt)
    dv = DN_HEADS * DN_DV
    fwd = lambda n: pl.BlockSpec((None, tt, n), lambda b, i: (b, i, 0))
    bwd = lambda n: pl.BlockSpec((None, tt, n), lambda b, i: (b, nt - 1 - i, 0))
    return pl.pallas_call(
        functools.partial(_dn_scan_kernel, tt=tt, c=c),
        grid=(b, nt),
        in_specs=[fwd(cols), bwd(cols), fwd(LANES), bwd(LANES),
                  pl.BlockSpec(pvec.shape, lambda b, i: (0, 0))],
        out_specs=[fwd(dv), bwd(dv)],
        out_shape=[jax.ShapeDtypeStruct((b, t_len, dv), F32)] * 2,
        scratch_shapes=[pltpu.VMEM((2 * DN_HEADS, DN_DK, DN_DV), F32)],
        compiler_params=pltpu.CompilerParams(
            dimension_semantics=("parallel", "arbitrary"), vmem_limit_bytes=VMEM_LIMIT),
        name="dn_scan",
    )(qkvn, qkvn, ab, ab, pvec)


def _attn_kernel(q_ref, kp_ref, kc_ref, kn_ref, qg_ref, kg_ref, sink_ref, out_ref,
                 kd_ref, vd_ref, bias_ref, *, tq, w):
    i = pl.program_id(1)
    nblk = pl.num_programs(1) * (tq // w)
    d = AT_HEAD_DIM
    lane = lax.broadcasted_iota(jnp.int32, (1, 2 * d), 1)
    lo = lane < d

    @pl.when(i == 0)
    def _():
        qi = lax.broadcasted_iota(jnp.int32, (w, 3 * w), 0)
        kj = lax.broadcasted_iota(jnp.int32, (w, 3 * w), 1)
        dist = jnp.abs(qi - (kj - w))
        for variant in range(4):
            ok = dist <= w
            if variant & 1:
                ok = ok & (kj >= w)
            if variant & 2:
                ok = ok & (kj < 2 * w)
            for h in range(AT_HEADS):
                slope = 2.0 ** (-ALIBI_MAX * (h + 1) / AT_HEADS)
                bias_ref[variant * AT_HEADS + h] = jnp.where(ok, (-slope * LOG2E) * dist.astype(F32), -jnp.inf)

    def head_rms(x, g):
        sq = x * x
        s_lo = jnp.sum(jnp.where(lo, sq, 0.0), axis=-1, keepdims=True)
        s_hi = jnp.sum(jnp.where(lo, 0.0, sq), axis=-1, keepdims=True)
        ms = jnp.where(lo, s_lo, s_hi) * (1.0 / d)
        return x * lax.rsqrt(ms + EPS) * g

    def fill_kv(src_ref, r0, rows):
        x = src_ref[...]
        kn = head_rms(x[:, :2 * d].astype(F32), kg_ref[...])
        vf = x[:, 2 * d:].astype(F32)
        k_roll = pltpu.roll(kn, d, axis=1)
        v_roll = pltpu.roll(vf, d, axis=1)
        kd_ref[0, r0:r0 + rows, :] = jnp.where(lo, kn, k_roll).astype(BF16)
        kd_ref[1, r0:r0 + rows, :] = jnp.where(lo, k_roll, kn).astype(BF16)
        vd_ref[0, r0:r0 + rows, :] = jnp.where(lo, vf, v_roll).astype(BF16)
        vd_ref[1, r0:r0 + rows, :] = jnp.where(lo, v_roll, vf).astype(BF16)

    fill_kv(kp_ref, 0, w)
    fill_kv(kc_ref, w, tq)
    fill_kv(kn_ref, w + tq, w)

    group = AT_HEADS // AT_KV_HEADS
    nsub = tq // w
    units = [(jb, g) for jb in range(nsub) for g in range(AT_KV_HEADS)]

    def scores(jb, g):
        qn = head_rms(q_ref[jb * w:(jb + 1) * w, g * 2 * d:(g + 1) * 2 * d].astype(F32),
                      qg_ref[...]) * (d ** -0.5 * LOG2E)
        lhs = jnp.concatenate([jnp.where(lo, qn, 0.0), jnp.where(lo, 0.0, qn)], axis=0)
        return _dot_nt(lhs.astype(BF16), kd_ref[g, jb * w:jb * w + 3 * w, :])

    ready = {}
    for idx, (jb, g) in enumerate(units):
        if idx % ATTN_AHEAD == 0:
            for nxt in range(idx, min(idx + ATTN_AHEAD, len(units))):
                ready[nxt] = scores(*units[nxt])
        s = ready.pop(idx)
        n = i * nsub + jb
        variant = None
        if jb == 0:
            variant = (n == 0).astype(jnp.int32)
        if jb == nsub - 1:
            at_end = 2 * (n == nblk - 1).astype(jnp.int32)
            variant = at_end if variant is None else variant + at_end
        ps = []
        inv = []
        for hh in range(group):
            h = g * group + hh
            sink = sink_ref[h] * LOG2E
            bias = bias_ref[h] if variant is None else bias_ref[variant * AT_HEADS + h]
            sh = s[hh * w:(hh + 1) * w] + bias
            m = jnp.maximum(jnp.max(sh, axis=-1, keepdims=True), sink)
            p = jnp.exp2(sh - m)
            den = jnp.sum(p, axis=-1, keepdims=True) + jnp.exp2(sink - m)
            ps.append(p.astype(BF16))
            inv.append(1.0 / den)
        pv = _dot(jnp.concatenate(ps, axis=0), vd_ref[g, jb * w:jb * w + 3 * w, :])
        o = jnp.where(lo, pv[:w] * inv[0], pv[w:] * inv[1])
        out_ref[jb * w:(jb + 1) * w, g * 2 * d:(g + 1) * 2 * d] = o.astype(BF16)


def _attn_call(q, kv, q_g2, k_g2, sink):
    b, t_len, qc = q.shape
    w = AT_WINDOW
    tq = min(ATTN_TILE, t_len)
    kvc = kv.shape[2]
    two_d = 2 * AT_HEAD_DIM
    return pl.pallas_call(
        functools.partial(_attn_kernel, tq=tq, w=w),
        grid=(b, t_len // tq),
        in_specs=[pl.BlockSpec((None, tq, qc), lambda b, i: (b, i, 0))]
        + _halo_specs(tq, w, t_len, kvc)
        + [pl.BlockSpec(q_g2.shape, lambda b, i: (0, 0)),
           pl.BlockSpec(k_g2.shape, lambda b, i: (0, 0)),
           pl.BlockSpec(memory_space=pltpu.SMEM)],
        out_specs=pl.BlockSpec((None, tq, qc), lambda b, i: (b, i, 0)),
        out_shape=jax.ShapeDtypeStruct((b, t_len, qc), BF16),
        scratch_shapes=[pltpu.VMEM((AT_KV_HEADS, tq + 2 * w, two_d), BF16),
                        pltpu.VMEM((AT_KV_HEADS, tq + 2 * w, two_d), BF16),
                        pltpu.VMEM((4 * AT_HEADS, w, 3 * w), F32)],
        compiler_params=pltpu.CompilerParams(
            dimension_semantics=("parallel", "arbitrary"), vmem_limit_bytes=VMEM_LIMIT),
        name="band_attn",
    )(q, kv, kv, kv, q_g2, k_g2, sink)


def _layer_params(l, ffn1_norm, ffn1_w_gate, ffn1_w_up, ffn1_w_down, mix_norm, w_in,
                  dn_conv, dn_a_log, dn_dt_bias, dn_out_norm, cv_dw, cv_dw_bias, cv_ln_g, cv_ln_b,
                  at_q_norm, at_k_norm, at_sink, w_out, ffn2_norm, ffn2_w_gate, ffn2_w_up,
                  ffn2_w_down, final_norm):
    row = lambda a: a[l].reshape(1, -1).astype(F32)
    bf = lambda a: a.astype(BF16)
    hk = DN_HEADS * DN_DK
    hv = DN_HEADS * DN_DV
    cv_ch = cv_dw.shape[2]
    sizes = (2 * hk + hv, hv, 2 * DN_HEADS, 2 * DN_HEADS, 2 * cv_ch,
             AT_HEADS * AT_HEAD_DIM, 2 * AT_KV_HEADS * AT_HEAD_DIM)
    offs = [0]
    for s in sizes:
        offs.append(offs[-1] + s)
    wi = w_in[l]
    piece = lambda n: wi[:, offs[n]:offs[n + 1]]
    n_gate = 4 * DN_HEADS
    w_ab = jnp.pad(jnp.concatenate([piece(2), piece(3)], axis=1), ((0, 0), (0, LANES - n_gate)))
    pad_gate = lambda a: jnp.pad(a.reshape(1, -1).astype(F32),
                                 ((0, 0), (2 * DN_HEADS, LANES - n_gate)))
    pvec = jnp.concatenate([pad_gate(dn_a_log[l]), pad_gate(dn_dt_bias[l])], axis=0)
    two = lambda a: jnp.concatenate([row(a), row(a)], axis=1)
    return dict(
        g1=row(ffn1_norm), wg1=bf(ffn1_w_gate[l]), wu1=bf(ffn1_w_up[l]), wd1=bf(ffn1_w_down[l]),
        gm=row(mix_norm), w_qkv=bf(piece(0)), w_z=bf(piece(1)), w_ab=bf(w_ab), w_glu=bf(piece(4)),
        w_q=bf(piece(5)), w_kv=bf(piece(6)),
        dn_conv=dn_conv[l].astype(F32), pvec=pvec, og=row(dn_out_norm),
        cv_dw=cv_dw[l].astype(F32), cv_b=row(cv_dw_bias), cv_g=row(cv_ln_g), cv_bb=row(cv_ln_b),
        q_g2=two(at_q_norm), k_g2=two(at_k_norm), sink=at_sink[l].astype(F32),
        w_out=bf(w_out[l]), g2=row(ffn2_norm), wg2=bf(ffn2_w_gate[l]), wu2=bf(ffn2_w_up[l]),
        wd2=bf(ffn2_w_down[l]), gf=row(final_norm))


def _layer(x, b, t_len, p):
    x1, qkv, z, glu, q_at, kv_at, ab = _ffn_in_call(x, p)
    seq = lambda a: a.reshape(b, t_len, a.shape[-1])
    flat = lambda a: a.reshape(b * t_len, a.shape[-1])
    qkvn, o_cv = _conv_mixers_call(seq(qkv), seq(glu), p['dn_conv'], p['cv_dw'], p['cv_b'],
                                   p['cv_g'], p['cv_bb'])
    o_f, o_b = _dn_scan_call(qkvn, seq(ab), p['pvec'])
    o_at = _attn_call(seq(q_at), seq(kv_at), p['q_g2'], p['k_g2'], p['sink'])
    return _out_ffn_call(x1, flat(o_f), flat(o_b), z, flat(o_cv), flat(o_at), p)


def kernel(x_prompt, x_sample, ffn1_norm, ffn1_w_gate, ffn1_w_up, ffn1_w_down, mix_norm, w_in, dn_conv, dn_a_log, dn_dt_bias, dn_out_norm, cv_dw, cv_dw_bias, cv_ln_g, cv_ln_b, at_q_norm, at_k_norm, at_sink, w_out, ffn2_norm, ffn2_w_gate, ffn2_w_up, ffn2_w_down, final_norm):
    params = (ffn1_norm, ffn1_w_gate, ffn1_w_up, ffn1_w_down, mix_norm, w_in,
              dn_conv, dn_a_log, dn_dt_bias, dn_out_norm, cv_dw, cv_dw_bias, cv_ln_g, cv_ln_b,
              at_q_norm, at_k_norm, at_sink, w_out, ffn2_norm, ffn2_w_gate, ffn2_w_up, ffn2_w_down,
              final_norm)
    depth = ffn1_norm.shape[0]
    layers = [_layer_params(l, *params) for l in range(depth)]

    def trunk(x):
        b, t_len, d = x.shape
        y = x.reshape(b * t_len, d)
        for p in layers:
            y = _layer(y, b, t_len, p)
        return y.reshape(b, t_len, d)

    return (trunk(x_prompt), trunk(x_sample))
```
